```python
import jax, jax.numpy as jnp
from jax import lax
import numpy as np

D_MODEL = 2048
BATCH = 2
SEQ = 8192
DEPTH = 1

CHUNK = 64
Q_BLOCK = 128
SB_HEADS = 8
SB_HEAD_DIM = 128
SB_WIDTH = SB_HEADS * SB_HEAD_DIM
SA_HEADS = 8
SA_HEAD_DIM = 128
SA_WIDTH = SA_HEADS * SA_HEAD_DIM
IDX_HEADS = 16
IDX_DIM = 64
TOPK_MAX = 256
RMS_EPS = 1e-6
IN_WIDTH = 4 * SB_WIDTH + 4 * SA_WIDTH + IDX_HEADS * IDX_DIM + IDX_DIM + IDX_HEADS + 2 * D_MODEL

kernel_name = "hybrid_stickbreak_dsa_gated_block"


def _split_points():
    sizes = [SB_WIDTH] * 4 + [SA_WIDTH] * 4 + [IDX_HEADS * IDX_DIM, IDX_DIM, IDX_HEADS, D_MODEL, D_MODEL]
    return [int(c) for c in np.cumsum(sizes)[:-1]]


def rms_norm(x, g):
    xf = x.astype(jnp.float32)
    y = xf * lax.rsqrt(jnp.mean(xf * xf, axis=-1, keepdims=True) + RMS_EPS)
    return (y * g.astype(jnp.float32)).astype(x.dtype)


def alibi_slopes(n_heads):
    return jnp.exp2(-8.0 * (jnp.arange(n_heads, dtype=jnp.float32) + 1.0) / n_heads)


def stick_breaking_attention(q, k, v):
    b, s, h, dh = q.shape
    scale = dh ** -0.5
    kpos = jnp.arange(s)

    def block(i):
        start = i * Q_BLOCK
        qb = lax.dynamic_slice_in_dim(q, start, Q_BLOCK, axis=1)
        qpos = start + jnp.arange(Q_BLOCK)
        z = jnp.einsum('bqhd,bshd->bhqs', qb, k).astype(jnp.float32) * scale
        strict = kpos[None, :] < qpos[:, None]
        log_keep = jnp.where(strict, jax.nn.log_sigmoid(-z), 0.0)
        after = lax.cumsum(log_keep, axis=3, reverse=True) - log_keep
        w = jnp.where(strict, jnp.exp(jax.nn.log_sigmoid(z) + after), 0.0)
        return jnp.einsum('bhqs,bshd->bqhd', w.astype(v.dtype), v)

    out = lax.map(block, jnp.arange(s // Q_BLOCK))
    return jnp.moveaxis(out, 0, 1).reshape(b, s, h, dh)


def indexed_sparse_attention(q, k, v, q_idx, k_idx, w_idx):
    b, s, h, dh = q.shape
    n_sel = min(TOPK_MAX, s // 4)
    kpos = jnp.arange(s)
    slopes = alibi_slopes(h)

    def block(i):
        start = i * Q_BLOCK
        qpos = start + jnp.arange(Q_BLOCK)
        visible = kpos[None, :] < (qpos[:, None] // CHUNK + 1) * CHUNK
        qi = lax.dynamic_slice_in_dim(q_idx, start, Q_BLOCK, axis=1)
        wi = lax.dynamic_slice_in_dim(w_idx, start, Q_BLOCK, axis=1).astype(jnp.float32) * IDX_HEADS ** -0.5
        dots = jnp.einsum('bqhd,bsd->bqhs', qi, k_idx).astype(jnp.float32) * IDX_DIM ** -0.5
        score = jnp.einsum('bqhs,bqh->bqs', jax.nn.relu(dots), wi)
        score = jnp.where(visible[None], score, -jnp.inf)
        top_val, top_idx = lax.top_k(score, n_sel)
        valid = jnp.isfinite(top_val)
        kg = jax.vmap(lambda kk, ii: kk[ii])(k, top_idx)
        vg = jax.vmap(lambda vv, ii: vv[ii])(v, top_idx)
        qb = lax.dynamic_slice_in_dim(q, start, Q_BLOCK, axis=1)
        logits = jnp.einsum('bqhd,bqkhd->bhqk', qb, kg).astype(jnp.float32) * dh ** -0.5
        dist = jnp.abs(qpos[None, :, None] - top_idx).astype(jnp.float32)
        logits = logits - slopes[None, :, None, None] * dist[:, None]
        logits = jnp.where(valid[:, None], logits, -jnp.inf)
        p = jax.nn.softmax(logits, axis=-1)
        return jnp.einsum('bhqk,bqkhd->bqhd', p.astype(v.dtype), vg)

    out = lax.map(block, jnp.arange(s // Q_BLOCK))
    return jnp.moveaxis(out, 0, 1).reshape(b, s, h, dh)


def setup_inputs(seed: int = 0) -> dict:
    key = jax.random.key(seed)
    ks = jax.random.split(key, 7)
    x = jax.random.normal(ks[0], (BATCH, SEQ, D_MODEL), jnp.float32)
    norm_gain = 1.0 + 0.01 * jax.random.normal(ks[1], (DEPTH, D_MODEL), jnp.float32)
    w_in = jax.random.normal(ks[2], (DEPTH, D_MODEL, IN_WIDTH), jnp.float32) * D_MODEL ** -0.5
    w_branch_sb = jax.random.normal(ks[3], (DEPTH, SB_WIDTH, D_MODEL), jnp.float32) * SB_WIDTH ** -0.5
    w_branch_sa = jax.random.normal(ks[4], (DEPTH, SA_WIDTH, D_MODEL), jnp.float32) * SA_WIDTH ** -0.5
    w_out = jax.random.normal(ks[5], (DEPTH, D_MODEL, D_MODEL), jnp.float32) * D_MODEL ** -0.5
    final_norm_gain = 1.0 + 0.01 * jax.random.normal(ks[6], (D_MODEL,), jnp.float32)
    return {"x": x, "norm_gain": norm_gain, "w_in": w_in, "w_branch_sb": w_branch_sb,
            "w_branch_sa": w_branch_sa, "w_out": w_out, "final_norm_gain": final_norm_gain}


def reference(x, norm_gain, w_in, w_branch_sb, w_branch_sa, w_out, final_norm_gain):
    b, s, _ = x.shape
    h = x
    splits = _split_points()
    for layer in range(DEPTH):
        u = rms_norm(h, norm_gain[layer])
        proj = jnp.einsum('bsd,de->bse', u, w_in[layer])
        (q_sb, k_sb, v_sb, z_sb, q_sa, k_sa, v_sa, z_sa,
         q_ix, k_ix, w_ix, g_sb, g_sa) = jnp.split(proj, splits, axis=-1)
        o_sb = stick_breaking_attention(q_sb.reshape(b, s, SB_HEADS, SB_HEAD_DIM),
                                        k_sb.reshape(b, s, SB_HEADS, SB_HEAD_DIM),
                                        v_sb.reshape(b, s, SB_HEADS, SB_HEAD_DIM))
        y_sb = jnp.einsum('bse,ed->bsd', o_sb.reshape(b, s, SB_WIDTH) * jax.nn.silu(z_sb), w_branch_sb[layer])
        o_sa = indexed_sparse_attention(q_sa.reshape(b, s, SA_HEADS, SA_HEAD_DIM),
                                        k_sa.reshape(b, s, SA_HEADS, SA_HEAD_DIM),
                                        v_sa.reshape(b, s, SA_HEADS, SA_HEAD_DIM),
                                        q_ix.reshape(b, s, IDX_HEADS, IDX_DIM), k_ix, w_ix)
        y_sa = jnp.einsum('bse,ed->bsd', o_sa.reshape(b, s, SA_WIDTH) * jax.nn.silu(z_sa), w_branch_sa[layer])
        mixed = jax.nn.sigmoid(g_sb) * y_sb + jax.nn.sigmoid(g_sa) * y_sa
        h = h + jnp.einsum('bsd,de->bse', mixed, w_out[layer])
    return rms_norm(h, final_norm_gain)
```

```python
import functools

import jax
import jax.numpy as jnp
from jax import lax
from jax.experimental import pallas as pl
from jax.experimental.pallas import tpu as pltpu

D_MODEL = 2048
CHUNK = 64
SB_HEADS = 8
SA_HEADS = 8
HEAD_DIM = 128
IDX_HEADS = 16
IDX_DIM = 64
TOPK_MAX = 256
RMS_EPS = 1e-6

LANES = 128
VMEM_LIMIT = 56 * 1024 * 1024

G_QSB, G_KSB, G_VSB, G_ZSB = 0, 8, 16, 24
G_QSA, G_KSA, G_VSA, G_ZSA = 32, 40, 48, 56
G_GSB, G_GSA, G_QIX = 64, 80, 96
N_GROUPS = 104
GROUPS_PER_TILE = 8

INT_MIN = -2 ** 31
MASKED_BITS = -228737632
MASKED = -1e30
SB_EXIT = -104.0

_f32 = jnp.float32
_bf16 = jnp.bfloat16
_i32 = jnp.int32


def _dot_nt(a, b):
    return lax.dot_general(a, b, (((1,), (1,)), ((), ())), preferred_element_type=_f32)


def _dot(a, b):
    return jnp.dot(a, b, preferred_element_type=_f32)


def _inproj_kernel(x_ref, g_ref, w_ref, wm_ref, o_ref, misc_ref, vt_ref, u_ref, *, tm, sub):
    n = pl.program_id(1)

    @pl.when(n == 0)
    def _():
        for r in range(tm // sub):
            x = x_ref[r * sub:(r + 1) * sub, :]
            ms = jnp.mean(x * x, axis=-1, keepdims=True)
            u = (x * lax.rsqrt(ms + RMS_EPS)) * g_ref[...]
            ub = u.astype(_bf16)
            u_ref[r * sub:(r + 1) * sub, :] = ub
            misc_ref[r * sub:(r + 1) * sub, :] = _dot(ub, wm_ref[...])

    for r in range(tm // sub):
        res = _dot(u_ref[r * sub:(r + 1) * sub, :], w_ref[...])
        for g in range(GROUPS_PER_TILE):
            o_ref[g, r * sub:(r + 1) * sub, :] = res[:, g * LANES:(g + 1) * LANES].astype(_bf16)

        @pl.when(n == G_VSA // GROUPS_PER_TILE)
        def _():
            for g in range(GROUPS_PER_TILE):
                vt_ref[g, r] = res[:, g * LANES:(g + 1) * LANES].T.astype(_bf16)


def _in_projection(x2, gain, w_main, w_misc, *, tm=1024, sub=256):
    m, d = x2.shape
    n_tiles = N_GROUPS // GROUPS_PER_TILE
    tn = GROUPS_PER_TILE * LANES
    return pl.pallas_call(
        functools.partial(_inproj_kernel, tm=tm, sub=sub),
        grid=(m // tm, n_tiles),
        in_specs=[
            pl.BlockSpec((tm, d), lambda i, j: (i, 0)),
            pl.BlockSpec((1, d), lambda i, j: (0, 0)),
            pl.BlockSpec((d, tn), lambda i, j: (0, j)),
            pl.BlockSpec((d, LANES), lambda i, j: (0, 0)),
        ],
        out_specs=[
            pl.BlockSpec((GROUPS_PER_TILE, tm, LANES), lambda i, j: (j, i, 0)),
            pl.BlockSpec((tm, LANES), lambda i, j: (i, 0)),
            pl.BlockSpec((SA_HEADS, tm // sub, LANES, sub), lambda i, j: (0, i, 0, 0)),
        ],
        out_shape=[
            jax.ShapeDtypeStruct((N_GROUPS, m, LANES), _bf16),
            jax.ShapeDtypeStruct((m, LANES), _f32),
            jax.ShapeDtypeStruct((SA_HEADS, m // sub, LANES, sub), _bf16),
        ],
        scratch_shapes=[pltpu.VMEM((tm, d), _bf16)],
        compiler_params=pltpu.CompilerParams(
            dimension_semantics=("arbitrary", "arbitrary"), vmem_limit_bytes=VMEM_LIMIT),
        name="in_projection",
    )(x2, gain, w_main, w_misc)


def _sb_kernel(q_ref, k_ref, v_ref, z_ref, o_ref, *, t, scale):
    i = pl.program_id(2)
    q = q_ref[0]
    row = lax.broadcasted_iota(_i32, (t, t), 0)
    col = lax.broadcasted_iota(_i32, (t, t), 1)
    tri = (row >= col).astype(_bf16)
    strict = col < row

    def block(j, carry, acc, diag):
        ks = pl.multiple_of(j * t, t)
        kb = k_ref[0, pl.ds(ks, t), :]
        vb = v_ref[0, pl.ds(ks, t), :]
        z = _dot_nt(q, kb) * scale
        lk = jnp.minimum(-z, 0.0) - jnp.log(1.0 + jnp.exp(-jnp.abs(z)))
        if diag:
            lk = jnp.where(strict, lk, 0.0)
        hi = lk.astype(_bf16)
        lo = (lk - hi.astype(_f32)).astype(_bf16)
        cum = _dot(hi, tri) + _dot(lo, tri)
        w = jnp.exp(z + cum + carry)
        if diag:
            w = jnp.where(strict, w, 0.0)
        acc = acc + _dot(w.astype(_bf16), vb)
        carry = carry + cum[:, 0:1]
        return carry, acc

    carry, acc = block(i, jnp.zeros((t, 1), _f32), jnp.zeros((t, HEAD_DIM), _f32), True)

    def cond(st):
        j, carry, _ = st
        return jnp.logical_and(j >= 0, jnp.max(carry) > SB_EXIT)

    def body(st):
        j, carry, acc = st
        carry, acc = block(j, carry, acc, False)
        return j - 1, carry, acc

    _, _, acc = lax.while_loop(cond, body, (i - 1, carry, acc))
    zg = z_ref[0].astype(_f32)
    o_ref[...] = (acc * (zg / (1.0 + jnp.exp(-zg)))).astype(o_ref.dtype)


def _sb_attention(proj, b, s, *, t=256):
    m = b * s
    nq = s // t
    return pl.pallas_call(
        functools.partial(_sb_kernel, t=t, scale=HEAD_DIM ** -0.5),
        grid=(b, SB_HEADS, nq),
        in_specs=[
            pl.BlockSpec((1, t, LANES), lambda bi, h, i: (G_QSB + h, bi * nq + i, 0)),
            pl.BlockSpec((1, s, LANES), lambda bi, h, i: (G_KSB + h, bi, 0)),
            pl.BlockSpec((1, s, LANES), lambda bi, h, i: (G_VSB + h, bi, 0)),
            pl.BlockSpec((1, t, LANES), lambda bi, h, i: (G_ZSB + h, bi * nq + i, 0)),
        ],
        out_specs=pl.BlockSpec((t, LANES), lambda bi, h, i: (bi * nq + i, h)),
        out_shape=jax.ShapeDtypeStruct((m, SB_HEADS * HEAD_DIM), _bf16),
        compiler_params=pltpu.CompilerParams(
            dimension_semantics=("arbitrary", "arbitrary", "arbitrary"),
            vmem_limit_bytes=VMEM_LIMIT),
        name="sb_attention",
    )(proj, proj, proj, proj)


def _sa_kernel(qi_ref, wq_ref, kx_ref, q_ref, k_ref, vt_ref, z_ref, o_ref, keys_ref,
               *, tq, tk, rc, nsel, scale):
    i = pl.program_id(1)
    nkb = (i + 1) * (tq // tk)
    nrc = nkb * (tk // rc)
    qpos = i * tq + lax.broadcasted_iota(_i32, (1, tq), 1)
    vis_end = (qpos // CHUNK + 1) * CHUNK

    w_t = wq_ref[...].T
    w_rows = [w_t[IDX_DIM + h:IDX_DIM + h + 1, :] * (IDX_HEADS ** -0.5 * IDX_DIM ** -0.5)
              for h in range(IDX_HEADS)]

    def score_block(c, _):
        ks = pl.multiple_of(c * tk, tk)
        acc = jnp.zeros((tk, tq), _f32)
        for h in range(IDX_HEADS):
            d = _dot_nt(kx_ref[h % 2, pl.ds(ks, tk), :], qi_ref[h // 2])
            acc = acc + jnp.maximum(d, 0.0) * w_rows[h]
        kpos = ks + lax.broadcasted_iota(_i32, (tk, 1), 0)
        bits = lax.bitcast_convert_type(acc, _i32)
        key = bits ^ ((bits >> 31) & 0x7FFFFFFF)
        keys_ref[pl.ds(ks, tk), :] = jnp.where(kpos < vis_end, key, INT_MIN)
        return 0

    lax.fori_loop(0, nkb, score_block, 0)

    def count(pred):
        def inner(r, acc):
            blk = keys_ref[pl.ds(pl.multiple_of(r * rc, rc), rc), :]
            return acc + jnp.where(pred(blk), 1, 0)
        acc = lax.fori_loop(0, nrc, inner, jnp.zeros((rc, tq), _i32))
        return jnp.sum(acc, axis=0, keepdims=True)

    def bisect(p, st):
        thr, cnt_thr = st
        cand = thr + lax.shift_left(jnp.int32(1), 31 - p)
        cnt = count(lambda blk: blk >= cand)
        ok = cnt >= nsel
        return jnp.where(ok, cand, thr), jnp.where(ok, cnt, cnt_thr)

    thr, cnt_thr = lax.fori_loop(
        0, 32, bisect, (jnp.full((1, tq), INT_MIN, _i32), jnp.full((1, tq), nsel, _i32)))

    tied = jnp.logical_and(cnt_thr > nsel, thr > INT_MIN)

    @pl.when(jnp.max(jnp.where(tied, 1, 0)) > 0)
    def _():
        need = (nsel - count(lambda blk: blk > thr)).astype(_f32)
        r_i = lax.broadcasted_iota(_i32, (tk, tk), 0)
        c_i = lax.broadcasted_iota(_i32, (tk, tk), 1)
        before = (c_i < r_i).astype(_bf16)

        def fix(c, run):
            ks = pl.multiple_of(c * tk, tk)
            blk = keys_ref[pl.ds(ks, tk), :]
            eq = blk == thr
            eqf = jnp.where(eq, 1.0, 0.0)
            rank = _dot(before, eqf.astype(_bf16)) + run
            keys_ref[pl.ds(ks, tk), :] = jnp.where(jnp.logical_and(eq, rank >= need), INT_MIN, blk)
            return run + jnp.sum(eqf, axis=0, keepdims=True)

        lax.fori_loop(0, nkb, fix, jnp.zeros((1, tq), _f32))

    thr_sel = jnp.maximum(thr, INT_MIN + 1)

    def to_mask(r, _):
        rows = pl.ds(pl.multiple_of(r * rc, rc), rc)
        keys_ref[rows, :] = jnp.where(keys_ref[rows, :] >= thr_sel, 0, MASKED_BITS)
        return 0

    lax.fori_loop(0, nrc, to_mask, 0)

    def head(h, _):
        qh = q_ref[h]
        slope = lax.bitcast_convert_type(
            jnp.full((1, tq), 126, _i32) - h << 23, _f32)

        def kblock(c, st):
            m_run, l_run, acc = st
            ks = pl.multiple_of(c * tk, tk)
            sc = _dot_nt(k_ref[h, pl.ds(ks, tk), :], qh)
            kpos = ks + lax.broadcasted_iota(_i32, (tk, 1), 0)
            dist = jnp.abs(qpos - kpos).astype(_f32)
            lg = sc * scale - slope * dist + lax.bitcast_convert_type(keys_ref[pl.ds(ks, tk), :], _f32)
            m_new = jnp.maximum(m_run, jnp.max(lg, axis=0, keepdims=True))
            alpha = jnp.exp(m_run - m_new)
            p = jnp.exp(lg - m_new)
            l_new = alpha * l_run + jnp.sum(p, axis=0, keepdims=True)
            acc = alpha * acc + _dot(vt_ref[h, c], p.astype(_bf16))
            return m_new, l_new, acc

        _, l_run, acc = lax.fori_loop(
            0, nkb, kblock,
            (jnp.full((1, tq), MASKED, _f32), jnp.zeros((1, tq), _f32), jnp.zeros((HEAD_DIM, tq), _f32)))
        out = (acc / l_run).T
        zg = z_ref[h].astype(_f32)
        o_ref[h] = (out * (zg / (1.0 + jnp.exp(-zg)))).astype(o_ref.dtype)
        return 0

    lax.fori_loop(0, SA_HEADS, head, 0)


def _sa_attention(proj, misc, kx, vt, b, s, *, tq=256, tk=256, rc=64):
    m = b * s
    nq = s // tq
    nsel = min(TOPK_MAX, s // 4)
    once = pl.Buffered(1)
    return pl.pallas_call(
        functools.partial(_sa_kernel, tq=tq, tk=tk, rc=rc, nsel=nsel, scale=HEAD_DIM ** -0.5),
        grid=(b, nq),
        in_specs=[
            pl.BlockSpec((8, tq, LANES), lambda bi, i: (G_QIX // 8, bi * nq + i, 0)),
            pl.BlockSpec((tq, LANES), lambda bi, i: (bi * nq + i, 0)),
            pl.BlockSpec((2, s, LANES), lambda bi, i: (0, bi, 0), pipeline_mode=once),
            pl.BlockSpec((8, tq, LANES), lambda bi, i: (G_QSA // 8, bi * nq + i, 0)),
            pl.BlockSpec((8, s, LANES), lambda bi, i: (G_KSA // 8, bi, 0), pipeline_mode=once),
            pl.BlockSpec((8, s // tk, LANES, tk), lambda bi, i: (0, bi, 0, 0), pipeline_mode=once),
            pl.BlockSpec((8, tq, LANES), lambda bi, i: (G_ZSA // 8, bi * nq + i, 0)),
        ],
        out_specs=pl.BlockSpec((8, tq, LANES), lambda bi, i: (0, bi * nq + i, 0)),
        out_shape=jax.ShapeDtypeStruct((SA_HEADS, m, LANES), _bf16),
        scratch_shapes=[pltpu.VMEM((s, tq), _i32)],
        compiler_params=pltpu.CompilerParams(
            dimension_semantics=("arbitrary", "arbitrary"), vmem_limit_bytes=VMEM_LIMIT),
        name="sa_attention",
    )(proj, misc, kx, proj, proj, vt, proj)


def _sigmoid(v):
    return 1.0 / (1.0 + jnp.exp(-v))


def _out_kernel(asb_ref, asa_ref, gsb_ref, gsa_ref, x_ref, wsb_ref, wsa_ref, wo_ref, fg_ref,
                o_ref, *, final):
    asa = jnp.concatenate([asa_ref[h] for h in range(SA_HEADS)], axis=1)
    ysb = _dot(asb_ref[...], wsb_ref[...])
    ysa = _dot(asa, wsa_ref[...])
    n_g = D_MODEL // LANES
    gsb = jnp.concatenate([gsb_ref[g] for g in range(n_g)], axis=1).astype(_f32)
    gsa = jnp.concatenate([gsa_ref[g] for g in range(n_g)], axis=1).astype(_f32)
    mixed = _sigmoid(gsb) * ysb + _sigmoid(gsa) * ysa
    hres = x_ref[...] + _dot(mixed.astype(_bf16), wo_ref[...])
    if final:
        ms = jnp.mean(hres * hres, axis=-1, keepdims=True)
        hres = (hres * lax.rsqrt(ms + RMS_EPS)) * fg_ref[...]
    o_ref[...] = hres


def _out_projection(a_sb, a_sa, proj, x2, w_sb, w_sa, w_o, fgain, *, final, tm=256):
    m, d = x2.shape
    n_g = d // LANES
    once = pl.Buffered(1)
    return pl.pallas_call(
        functools.partial(_out_kernel, final=final),
        grid=(m // tm,),
        in_specs=[
            pl.BlockSpec((tm, SB_HEADS * HEAD_DIM), lambda i: (i, 0)),
            pl.BlockSpec((SA_HEADS, tm, LANES), lambda i: (0, i, 0)),
            pl.BlockSpec((n_g, tm, LANES), lambda i: (G_GSB // n_g, i, 0)),
            pl.BlockSpec((n_g, tm, LANES), lambda i: (G_GSA // n_g, i, 0)),
            pl.BlockSpec((tm, d), lambda i: (i, 0)),
            pl.BlockSpec(w_sb.shape, lambda i: (0, 0), pipeline_mode=once),
            pl.BlockSpec(w_sa.shape, lambda i: (0, 0), pipeline_mode=once),
            pl.BlockSpec(w_o.shape, lambda i: (0, 0), pipeline_mode=once),
            pl.BlockSpec((1, d), lambda i: (0, 0)),
        ],
        out_specs=pl.BlockSpec((tm, d), lambda i: (i, 0)),
        out_shape=jax.ShapeDtypeStruct((m, d), _f32),
        compiler_params=pltpu.CompilerParams(
            dimension_semantics=("arbitrary",), vmem_limit_bytes=VMEM_LIMIT),
        name="out_projection",
    )(a_sb, a_sa, proj, proj, x2, w_sb, w_sa, w_o, fgain)


def _split_in_weights(w):
    c_qix = 4 * SB_HEADS * HEAD_DIM + 4 * SA_HEADS * HEAD_DIM
    c_kix = c_qix + IDX_HEADS * IDX_DIM
    c_gsb = c_kix + IDX_DIM + IDX_HEADS
    w_main = jnp.concatenate([w[:, :c_qix], w[:, c_gsb:], w[:, c_qix:c_kix]], axis=1)
    w_misc = jnp.pad(w[:, c_kix:c_gsb], ((0, 0), (0, LANES - (c_gsb - c_kix))))
    return w_main.astype(_bf16), w_misc.astype(_bf16)


def kernel(x, norm_gain, w_in, w_branch_sb, w_branch_sa, w_out, final_norm_gain):
    b, s, d = x.shape
    depth = norm_gain.shape[0]
    h2 = x.reshape(b * s, d)
    fgain = final_norm_gain.reshape(1, d)
    for layer in range(depth):
        w_main, w_misc = _split_in_weights(w_in[layer])
        proj, misc, vt = _in_projection(h2, norm_gain[layer].reshape(1, d), w_main, w_misc)
        kix = misc[:, :IDX_DIM].astype(_bf16)
        zeros = jnp.zeros_like(kix)
        kx = jnp.stack([jnp.concatenate([kix, zeros], axis=1),
                        jnp.concatenate([zeros, kix], axis=1)])
        a_sb = _sb_attention(proj, b, s)
        a_sa = _sa_attention(proj, misc, kx, vt, b, s)
        h2 = _out_projection(a_sb, a_sa, proj, h2,
                             w_branch_sb[layer].astype(_bf16), w_branch_sa[layer].astype(_bf16),
                             w_out[layer].astype(_bf16), fgain, final=(layer == depth - 1))
    return h2.reshape(b, s, d)
```

```python
import functools

import jax
import jax.numpy as jnp
from jax import lax
from jax.experimental import pallas as pl
from jax.experimental.pallas import tpu as pltpu

D_MODEL = 2048
CHUNK = 64
SB_HEADS = 8
SA_HEADS = 8
HEAD_DIM = 128
IDX_HEADS = 16
IDX_DIM = 64
TOPK_MAX = 256
RMS_EPS = 1e-6

LANES = 128
VMEM_LIMIT = 58 * 1024 * 1024

G_QSB, G_KSB, G_VSB, G_ZSB = 0, 8, 16, 24
G_QSA, G_KSA, G_VSA, G_ZSA = 32, 40, 48, 56
G_GSB, G_GSA, G_QIX = 64, 80, 96
N_GROUPS = 104
GROUPS_PER_TILE = 8

INT_MIN = -2 ** 31
KEY_NEG_INF = INT_MIN + 0x7FFFFF
F32_LOWEST = -3.4028234663852886e38
MASKED = -1e30
ATTN_SCALE = HEAD_DIM ** -0.5
SB_EXIT = -104.0

_f32 = jnp.float32
_bf16 = jnp.bfloat16
_i32 = jnp.int32


def _dot_nt(a, b):
    return lax.dot_general(a, b, (((1,), (1,)), ((), ())), preferred_element_type=_f32)


def _dot(a, b):
    return jnp.dot(a, b, preferred_element_type=_f32)


def _inproj_kernel(x_ref, g_ref, w_ref, wm_ref, o_ref, misc_ref, vt_ref, u_ref, *, tm, sub):
    n = pl.program_id(1)

    @pl.when(n == 0)
    def _():
        for r in range(tm // sub):
            x = x_ref[r * sub:(r + 1) * sub, :]
            ms = jnp.mean(x * x, axis=-1, keepdims=True)
            u = (x * lax.rsqrt(ms + RMS_EPS)) * g_ref[...]
            ub = u.astype(_bf16)
            u_ref[r * sub:(r + 1) * sub, :] = ub
            misc_ref[r * sub:(r + 1) * sub, :] = _dot(ub, wm_ref[...])

    is_query = jnp.logical_or(n == G_QSB // GROUPS_PER_TILE, n == G_QSA // GROUPS_PER_TILE)
    mult = jnp.where(is_query, ATTN_SCALE, 1.0).astype(_f32)
    for r in range(tm // sub):
        res = _dot(u_ref[r * sub:(r + 1) * sub, :], w_ref[...]) * mult
        for g in range(GROUPS_PER_TILE):
            o_ref[g, r * sub:(r + 1) * sub, :] = res[:, g * LANES:(g + 1) * LANES].astype(_bf16)

        @pl.when(n == G_VSA // GROUPS_PER_TILE)
        def _():
            for g in range(GROUPS_PER_TILE):
                vt_ref[g, r] = res[:, g * LANES:(g + 1) * LANES].T.astype(_bf16)


def _in_projection(x2, gain, w_main, w_misc, *, tm=1024, sub=256):
    m, d = x2.shape
    n_tiles = N_GROUPS // GROUPS_PER_TILE
    tn = GROUPS_PER_TILE * LANES
    return pl.pallas_call(
        functools.partial(_inproj_kernel, tm=tm, sub=sub),
        grid=(m // tm, n_tiles),
        in_specs=[
            pl.BlockSpec((tm, d), lambda i, j: (i, 0)),
            pl.BlockSpec((1, d), lambda i, j: (0, 0)),
            pl.BlockSpec((d, tn), lambda i, j: (0, j)),
            pl.BlockSpec((d, LANES), lambda i, j: (0, 0)),
        ],
        out_specs=[
            pl.BlockSpec((GROUPS_PER_TILE, tm, LANES), lambda i, j: (j, i, 0)),
            pl.BlockSpec((tm, LANES), lambda i, j: (i, 0)),
            pl.BlockSpec((SA_HEADS, tm // sub, LANES, sub), lambda i, j: (0, i, 0, 0)),
        ],
        out_shape=[
            jax.ShapeDtypeStruct((N_GROUPS, m, LANES), _bf16),
            jax.ShapeDtypeStruct((m, LANES), _f32),
            jax.ShapeDtypeStruct((SA_HEADS, m // sub, LANES, sub), _bf16),
        ],
        scratch_shapes=[pltpu.VMEM((tm, d), _bf16)],
        compiler_params=pltpu.CompilerParams(
            dimension_semantics=("arbitrary", "arbitrary"), vmem_limit_bytes=VMEM_LIMIT),
        name="in_projection",
    )(x2, gain, w_main, w_misc)


def _sb_kernel(q_ref, k_ref, v_ref, z_ref, o_ref, *, t):
    i = pl.program_id(2)
    q = q_ref[0]
    row = lax.broadcasted_iota(_i32, (t, t), 0)
    col = lax.broadcasted_iota(_i32, (t, t), 1)
    tri = (row >= col).astype(_bf16)
    strict = col < row

    def block(j, carry, acc, diag):
        ks = pl.multiple_of(j * t, t)
        kb = k_ref[0, pl.ds(ks, t), :]
        vb = v_ref[0, pl.ds(ks, t), :]
        z = _dot_nt(q, kb)
        lk = jnp.minimum(-z, 0.0) - jnp.log(1.0 + jnp.exp(-jnp.abs(z)))
        if diag:
            lk = jnp.where(strict, lk, 0.0)
        hi = lk.astype(_bf16)
        lo = (lk - hi.astype(_f32)).astype(_bf16)
        cum = _dot(hi, tri) + _dot(lo, tri)
        w = jnp.exp(z + cum + carry)
        if diag:
            w = jnp.where(strict, w, 0.0)
        acc = acc + _dot(w.astype(_bf16), vb)
        carry = carry + cum[:, 0:1]
        return carry, acc

    carry, acc = block(i, jnp.zeros((t, 1), _f32), jnp.zeros((t, HEAD_DIM), _f32), True)

    def cond(st):
        j, carry, _ = st
        return jnp.logical_and(j >= 0, jnp.max(carry) > SB_EXIT)

    def body(st):
        j, carry, acc = st
        carry, acc = block(j, carry, acc, False)
        return j - 1, carry, acc

    _, _, acc = lax.while_loop(cond, body, (i - 1, carry, acc))
    zg = z_ref[0].astype(_f32)
    o_ref[...] = (acc * (zg / (1.0 + jnp.exp(-zg)))).astype(o_ref.dtype)


def _sb_attention(proj, b, s, *, t=256):
    m = b * s
    nq = s // t
    return pl.pallas_call(
        functools.partial(_sb_kernel, t=t),
        grid=(b, SB_HEADS, nq),
        in_specs=[
            pl.BlockSpec((1, t, LANES), lambda bi, h, i: (G_QSB + h, bi * nq + i, 0)),
            pl.BlockSpec((1, s, LANES), lambda bi, h, i: (G_KSB + h, bi, 0)),
            pl.BlockSpec((1, s, LANES), lambda bi, h, i: (G_VSB + h, bi, 0)),
            pl.BlockSpec((1, t, LANES), lambda bi, h, i: (G_ZSB + h, bi * nq + i, 0)),
        ],
        out_specs=pl.BlockSpec((t, LANES), lambda bi, h, i: (bi * nq + i, h)),
        out_shape=jax.ShapeDtypeStruct((m, SB_HEADS * HEAD_DIM), _bf16),
        compiler_params=pltpu.CompilerParams(
            dimension_semantics=("arbitrary", "arbitrary", "arbitrary"),
            vmem_limit_bytes=VMEM_LIMIT),
        name="sb_attention",
    )(proj, proj, proj, proj)


def _sa_kernel(qi_ref, wq_ref, kx_ref, q_ref, k_ref, ka_ref, vt_ref, z_ref, o_ref,
               sc_ref, qsw_ref, qcat_ref, ahead_ref, *, t, rc, wide, nsel):
    i = pl.program_id(1)
    nkb = i + 1
    qpos = i * t + lax.broadcasted_iota(_i32, (1, t), 1)
    vis_end = (qpos // CHUNK + 1) * CHUNK
    krow = lax.broadcasted_iota(_i32, (t, 1), 0)

    w_t = wq_ref[...].T
    w_rows = [w_t[IDX_DIM + h:IDX_DIM + h + 1, :] * (IDX_HEADS ** -0.5 * IDX_DIM ** -0.5)
              for h in range(IDX_HEADS)]
    for g in range(IDX_HEADS // 2):
        qsw_ref[g] = pltpu.roll(qi_ref[g], IDX_DIM, 1)

    def score_block(c, _):
        ks = pl.multiple_of(c * t, t)
        kxb = kx_ref[pl.ds(ks, t), :]
        acc = jnp.zeros((t, t), _f32)
        for h in range(IDX_HEADS):
            qh = qi_ref[h // 2] if h % 2 == 0 else qsw_ref[h // 2]
            acc = acc + jnp.maximum(_dot_nt(kxb, qh), 0.0) * w_rows[h]
        sc_ref[pl.ds(ks, t), :] = jnp.where(ks + krow < vis_end, acc, -jnp.inf)
        return 0

    lax.fori_loop(0, nkb, score_block, 0)

    def key_to_float(key):
        key = jnp.maximum(key, KEY_NEG_INF)
        return lax.bitcast_convert_type(key ^ ((key >> 31) & 0x7FFFFFFF), _f32)

    def count(pred):
        def inner(c, acc):
            ks = pl.multiple_of(c * t, t)
            for u in range(t // rc):
                acc = acc + jnp.where(pred(sc_ref[pl.ds(ks + u * rc, rc), :]), 1, 0)
            return acc
        acc = lax.fori_loop(0, nkb, inner, jnp.zeros((rc, t), _i32))
        return jnp.sum(acc, axis=0, keepdims=True)

    def bisect(p, st):
        key, cnt_key = st
        cand = key + lax.shift_left(jnp.int32(1), 31 - p)
        cand_f = key_to_float(cand)
        cnt = count(lambda blk: blk >= cand_f)
        ok = cnt >= nsel
        return jnp.where(ok, cand, key), jnp.where(ok, cnt, cnt_key)

    key, cnt_key = lax.fori_loop(
        0, 32, bisect, (jnp.full((1, t), INT_MIN, _i32), jnp.full((1, t), nsel, _i32)))
    thr = key_to_float(key)

    tied = jnp.logical_and(cnt_key > nsel, thr > -jnp.inf)

    @pl.when(jnp.max(jnp.where(tied, 1, 0)) > 0)
    def _():
        need = (nsel - count(lambda blk: blk > thr)).astype(_f32)
        r_i = lax.broadcasted_iota(_i32, (t, t), 0)
        c_i = lax.broadcasted_iota(_i32, (t, t), 1)
        before = (c_i < r_i).astype(_bf16)

        def fix(c, run):
            ks = pl.multiple_of(c * t, t)
            blk = sc_ref[pl.ds(ks, t), :]
            eq = blk == thr
            eqf = jnp.where(eq, 1.0, 0.0)
            rank = _dot(before, eqf.astype(_bf16)) + run
            sc_ref[pl.ds(ks, t), :] = jnp.where(jnp.logical_and(eq, rank >= need), -jnp.inf, blk)
            return run + jnp.sum(eqf, axis=0, keepdims=True)

        lax.fori_loop(0, nkb, fix, jnp.zeros((1, t), _f32))

    thr_sel = jnp.maximum(thr, F32_LOWEST)

    def to_mask(c, _):
        ks = pl.multiple_of(c * t, t)
        for u in range(t // rc):
            rows = pl.ds(ks + u * rc, rc)
            sc_ref[rows, :] = jnp.where(sc_ref[rows, :] >= thr_sel, 0.0, MASKED)
        return 0

    lax.fori_loop(0, nkb, to_mask, 0)

    lane = lax.broadcasted_iota(_i32, (t, LANES), 1)
    tpos = i * t + lax.broadcasted_iota(_i32, (t, LANES), 0)
    a_t = (tpos >> 6).astype(_f32)
    b_t = (tpos & (CHUNK - 1)).astype(_f32)
    for h in range(SA_HEADS):
        m_h = 2.0 ** -(h + 1)
        qa = jnp.where(lane == 0, CHUNK * m_h,
                       jnp.where(lane == 1, m_h,
                                 jnp.where(lane == 2, -CHUNK * m_h * a_t,
                                           jnp.where(lane == 3, -m_h * b_t, 0.0))))
        qcat_ref[h] = jnp.concatenate([q_ref[h], qa.astype(_bf16)], axis=1)
    wt = wide * t
    n_span = lax.shift_right_logical(i + wide, wide.bit_length() - 1)
    last = n_span - 1

    def mask_tail(c, _):
        sc_ref[pl.ds(pl.multiple_of(c * t, t), t), :] = jnp.full((t, t), MASKED, _f32)
        return 0

    lax.fori_loop(nkb, n_span * wide, mask_tail, 0)
    ahead_ref[...] = jnp.maximum(
        last * wt + lax.broadcasted_iota(_i32, (wt, 1), 0) - qpos, 0).astype(_f32)

    def head_pair(hp, _):
        heads = (hp, hp + SA_HEADS // 2)

        def span(c, states, is_last):
            ks = pl.multiple_of(c * wt, wt)
            logits = [[None] * wide for _ in heads]
            m_new = [st[0] for st in states]
            for j in range(wide):
                rows = pl.ds(ks + j * t, t)
                ka = ka_ref[rows, :]
                for a, h in enumerate(heads):
                    lhs = jnp.concatenate([k_ref[h, rows, :], ka], axis=1)
                    lg = _dot_nt(lhs, qcat_ref[h]) + sc_ref[rows, :]
                    if is_last:
                        two_m = lax.bitcast_convert_type(jnp.full((1, t), 127, _i32) - h << 23, _f32)
                        lg = lg - two_m * ahead_ref[j * t:(j + 1) * t, :]
                    logits[a][j] = lg
                    m_new[a] = jnp.maximum(m_new[a], jnp.max(lg, axis=0, keepdims=True))
            alpha = [jnp.exp(st[0] - m) for st, m in zip(states, m_new)]
            l_new = [al * st[1] for st, al in zip(states, alpha)]
            acc = [al * st[2] for st, al in zip(states, alpha)]
            for j in range(wide):
                for a, h in enumerate(heads):
                    p = jnp.exp(logits[a][j] - m_new[a])
                    l_new[a] = l_new[a] + jnp.sum(p, axis=0, keepdims=True)
                    acc[a] = acc[a] + _dot(vt_ref[h, c * wide + j], p.astype(_bf16))
            return tuple(zip(m_new, l_new, acc))

        init = (jnp.full((1, t), MASKED, _f32), jnp.zeros((1, t), _f32), jnp.zeros((HEAD_DIM, t), _f32))
        states = lax.fori_loop(0, last, lambda c, st: span(c, st, False), (init, init))
        states = span(last, states, True)
        for h, (_, l_run, acc) in zip(heads, states):
            out = (acc * (1.0 / l_run)).T
            zg = z_ref[h].astype(_f32)
            o_ref[h] = (out * (zg / (1.0 + jnp.exp(-zg)))).astype(o_ref.dtype)
        return 0

    lax.fori_loop(0, SA_HEADS // 2, head_pair, 0)


def _sa_attention(proj, misc, kx, ka, vt, b, s, *, t=256, rc=64, wide=4):
    m = b * s
    nq = s // t
    nsel = min(TOPK_MAX, s // 4)
    once = pl.Buffered(1)
    return pl.pallas_call(
        functools.partial(_sa_kernel, t=t, rc=rc, wide=wide, nsel=nsel),
        grid=(b, nq),
        in_specs=[
            pl.BlockSpec((8, t, LANES), lambda bi, i: (G_QIX // 8, bi * nq + i, 0)),
            pl.BlockSpec((t, LANES), lambda bi, i: (bi * nq + i, 0)),
            pl.BlockSpec((s, LANES), lambda bi, i: (bi, 0), pipeline_mode=once),
            pl.BlockSpec((8, t, LANES), lambda bi, i: (G_QSA // 8, bi * nq + i, 0)),
            pl.BlockSpec((8, s, LANES), lambda bi, i: (G_KSA // 8, bi, 0), pipeline_mode=once),
            pl.BlockSpec((s, LANES), lambda bi, i: (0, 0), pipeline_mode=once),
            pl.BlockSpec((8, s // t, LANES, t), lambda bi, i: (0, bi, 0, 0), pipeline_mode=once),
            pl.BlockSpec((8, t, LANES), lambda bi, i: (G_ZSA // 8, bi * nq + i, 0)),
        ],
        out_specs=pl.BlockSpec((8, t, LANES), lambda bi, i: (0, bi * nq + i, 0)),
        out_shape=jax.ShapeDtypeStruct((SA_HEADS, m, LANES), _bf16),
        scratch_shapes=[
            pltpu.VMEM((s, t), _f32),
            pltpu.VMEM((IDX_HEADS // 2, t, LANES), _bf16),
            pltpu.VMEM((SA_HEADS, t, 2 * LANES), _bf16),
            pltpu.VMEM((wide * t, t), _f32),
        ],
        compiler_params=pltpu.CompilerParams(
            dimension_semantics=("arbitrary", "arbitrary"), vmem_limit_bytes=VMEM_LIMIT),
        name="sa_attention",
    )(proj, misc, kx, proj, proj, ka, vt, proj)


def _sigmoid(v):
    return 1.0 / (1.0 + jnp.exp(-v))


def _out_kernel(asb_ref, asa_ref, gsb_ref, gsa_ref, x_ref, wsb_ref, wsa_ref, wo_ref, fg_ref,
                o_ref, *, final):
    asa = jnp.concatenate([asa_ref[h] for h in range(SA_HEADS)], axis=1)
    ysb = _dot(asb_ref[...], wsb_ref[...])
    ysa = _dot(asa, wsa_ref[...])
    n_g = D_MODEL // LANES
    gsb = jnp.concatenate([gsb_ref[g] for g in range(n_g)], axis=1).astype(_f32)
    gsa = jnp.concatenate([gsa_ref[g] for g in range(n_g)], axis=1).astype(_f32)
    mixed = _sigmoid(gsb) * ysb + _sigmoid(gsa) * ysa
    hres = x_ref[...] + _dot(mixed.astype(_bf16), wo_ref[...])
    if final:
        ms = jnp.mean(hres * hres, axis=-1, keepdims=True)
        hres = (hres * lax.rsqrt(ms + RMS_EPS)) * fg_ref[...]
    o_ref[...] = hres


def _out_projection(a_sb, a_sa, proj, x2, w_sb, w_sa, w_o, fgain, *, final, tm=256):
    m, d = x2.shape
    n_g = d // LANES
    once = pl.Buffered(1)
    return pl.pallas_call(
        functools.partial(_out_kernel, final=final),
        grid=(m // tm,),
        in_specs=[
            pl.BlockSpec((tm, SB_HEADS * HEAD_DIM), lambda i: (i, 0)),
            pl.BlockSpec((SA_HEADS, tm, LANES), lambda i: (0, i, 0)),
            pl.BlockSpec((n_g, tm, LANES), lambda i: (G_GSB // n_g, i, 0)),
            pl.BlockSpec((n_g, tm, LANES), lambda i: (G_GSA // n_g, i, 0)),
            pl.BlockSpec((tm, d), lambda i: (i, 0)),
            pl.BlockSpec(w_sb.shape, lambda i: (0, 0), pipeline_mode=once),
            pl.BlockSpec(w_sa.shape, lambda i: (0, 0), pipeline_mode=once),
            pl.BlockSpec(w_o.shape, lambda i: (0, 0), pipeline_mode=once),
            pl.BlockSpec((1, d), lambda i: (0, 0)),
        ],
        out_specs=pl.BlockSpec((tm, d), lambda i: (i, 0)),
        out_shape=jax.ShapeDtypeStruct((m, d), _f32),
        compiler_params=pltpu.CompilerParams(
            dimension_semantics=("arbitrary",), vmem_limit_bytes=VMEM_LIMIT),
        name="out_projection",
    )(a_sb, a_sa, proj, proj, x2, w_sb, w_sa, w_o, fgain)


def _split_in_weights(w):
    c_qix = 4 * SB_HEADS * HEAD_DIM + 4 * SA_HEADS * HEAD_DIM
    c_kix = c_qix + IDX_HEADS * IDX_DIM
    c_gsb = c_kix + IDX_DIM + IDX_HEADS
    w_main = jnp.concatenate([w[:, :c_qix], w[:, c_gsb:], w[:, c_qix:c_kix]], axis=1)
    w_misc = jnp.pad(w[:, c_kix:c_gsb], ((0, 0), (0, LANES - (c_gsb - c_kix))))
    return w_main.astype(_bf16), w_misc.astype(_bf16)


def _alibi_key_columns(s):
    pos = jnp.arange(s, dtype=_i32)
    cols = jnp.stack([pos // CHUNK, pos % CHUNK, jnp.ones_like(pos), jnp.ones_like(pos)], axis=1)
    return jnp.pad(cols, ((0, 0), (0, LANES - 4))).astype(_bf16)


def kernel(x, norm_gain, w_in, w_branch_sb, w_branch_sa, w_out, final_norm_gain):
    b, s, d = x.shape
    depth = norm_gain.shape[0]
    h2 = x.reshape(b * s, d)
    fgain = final_norm_gain.reshape(1, d)
    for layer in range(depth):
        w_main, w_misc = _split_in_weights(w_in[layer])
        proj, misc, vt = _in_projection(h2, norm_gain[layer].reshape(1, d), w_main, w_misc)
        kx = jnp.pad(misc[:, :IDX_DIM], ((0, 0), (0, LANES - IDX_DIM))).astype(_bf16)
        a_sb = _sb_attention(proj, b, s)
        a_sa = _sa_attention(proj, misc, kx, _alibi_key_columns(s), vt, b, s)
        h2 = _out_projection(a_sb, a_sa, proj, h2,
                             w_branch_sb[layer].astype(_bf16), w_branch_sa[layer].astype(_bf16),
                             w_out[layer].astype(_bf16), fgain, final=(layer == depth - 1))
    return h2.reshape(b, s, d)
```

```python
import functools

import jax
import jax.numpy as jnp
from jax import lax
from jax.experimental import pallas as pl
from jax.experimental.pallas import tpu as pltpu

D_MODEL = 2048
CHUNK = 64
SB_HEADS = 8
SA_HEADS = 8
HEAD_DIM = 128
IDX_HEADS = 16
IDX_DIM = 64
TOPK_MAX = 256
RMS_EPS = 1e-6

LANES = 128
VMEM_LIMIT = 58 * 1024 * 1024

G_QSB, G_KSB, G_VSB, G_ZSB = 0, 8, 16, 24
G_QSA, G_KSA, G_VSA, G_ZSA = 32, 40, 48, 56
G_GSB, G_GSA, G_QIX = 64, 80, 96
N_GROUPS = 104
GROUPS_PER_TILE = 8

INT_MIN = -2 ** 31
KEY_NEG_INF = INT_MIN + 0x7FFFFF
F32_LOWEST = -3.4028234663852886e38
MASKED = -1e30
ATTN_SCALE = HEAD_DIM ** -0.5
SB_EXIT = -104.0
SOFTMAX_FLOOR = 1e-17

_f32 = jnp.float32
_bf16 = jnp.bfloat16
_i32 = jnp.int32


def _dot_nt(a, b):
    return lax.dot_general(a, b, (((1,), (1,)), ((), ())), preferred_element_type=_f32)


def _dot(a, b):
    return jnp.dot(a, b, preferred_element_type=_f32)


def _inproj_kernel(x_ref, g_ref, w_ref, wm_ref, o_ref, misc_ref, vt_ref, u_ref, *, tm, sub, dsub):
    n = pl.program_id(1)

    @pl.when(n == 0)
    def _():
        for r in range(tm // sub):
            x = x_ref[r * sub:(r + 1) * sub, :]
            ms = jnp.mean(x * x, axis=-1, keepdims=True)
            u = (x * lax.rsqrt(ms + RMS_EPS)) * g_ref[...]
            ub = u.astype(_bf16)
            u_ref[r * sub:(r + 1) * sub, :] = ub
            misc_ref[r * sub:(r + 1) * sub, :] = _dot(ub, wm_ref[...])

    is_query = jnp.logical_or(n == G_QSB // GROUPS_PER_TILE, n == G_QSA // GROUPS_PER_TILE)
    mult = jnp.where(is_query, ATTN_SCALE, 1.0).astype(_f32)
    for r in range(tm // dsub):
        res = _dot(u_ref[r * dsub:(r + 1) * dsub, :], w_ref[...]) * mult
        for g in range(GROUPS_PER_TILE):
            o_ref[g, r * dsub:(r + 1) * dsub, :] = res[:, g * LANES:(g + 1) * LANES].astype(_bf16)

        @pl.when(n == G_VSA // GROUPS_PER_TILE)
        def _():
            for g in range(GROUPS_PER_TILE):
                for q in range(dsub // sub):
                    vt_ref[g, r * (dsub // sub) + q] = (
                        res[q * sub:(q + 1) * sub, g * LANES:(g + 1) * LANES].T.astype(_bf16))


def _in_projection(x2, gain, w_main, w_misc, *, tm=1024, sub=256, dsub=1024):
    m, d = x2.shape
    n_tiles = N_GROUPS // GROUPS_PER_TILE
    tn = GROUPS_PER_TILE * LANES
    return pl.pallas_call(
        functools.partial(_inproj_kernel, tm=tm, sub=sub, dsub=dsub),
        grid=(m // tm, n_tiles),
        in_specs=[
            pl.BlockSpec((tm, d), lambda i, j: (i, 0)),
            pl.BlockSpec((1, d), lambda i, j: (0, 0)),
            pl.BlockSpec((d, tn), lambda i, j: (0, j)),
            pl.BlockSpec((d, LANES), lambda i, j: (0, 0)),
        ],
        out_specs=[
            pl.BlockSpec((GROUPS_PER_TILE, tm, LANES), lambda i, j: (j, i, 0)),
            pl.BlockSpec((tm, LANES), lambda i, j: (i, 0)),
            pl.BlockSpec((SA_HEADS, tm // sub, LANES, sub), lambda i, j: (0, i, 0, 0)),
        ],
        out_shape=[
            jax.ShapeDtypeStruct((N_GROUPS, m, LANES), _bf16),
            jax.ShapeDtypeStruct((m, LANES), _f32),
            jax.ShapeDtypeStruct((SA_HEADS, m // sub, LANES, sub), _bf16),
        ],
        scratch_shapes=[pltpu.VMEM((tm, d), _bf16)],
        compiler_params=pltpu.CompilerParams(
            dimension_semantics=("arbitrary", "arbitrary"), vmem_limit_bytes=VMEM_LIMIT),
        name="in_projection",
    )(x2, gain, w_main, w_misc)


def _sb_kernel(q_ref, k_ref, v_ref, z_ref, o_ref, *, t, hg):
    i = pl.program_id(2)
    row = lax.broadcasted_iota(_i32, (t, t), 0)
    col = lax.broadcasted_iota(_i32, (t, t), 1)
    tri = (row >= col).astype(_bf16)
    strict = col < row
    heads = range(hg)

    def scores(h, j):
        return _dot_nt(q_ref[h], k_ref[h, pl.ds(pl.multiple_of(j * t, t), t), :])

    def log_keep(z):
        return jnp.minimum(-z, 0.0) - jnp.log(1.0 + jnp.exp(-jnp.abs(z)))

    def suffix_sum(lk):
        hi = lk.astype(_bf16)
        lo = (lk - hi.astype(_f32)).astype(_bf16)
        return _dot(hi, tri) + _dot(lo, tri)

    def values(h, j):
        return v_ref[h, pl.ds(pl.multiple_of(j * t, t), t), :]

    has_prev = i > 0
    jp = jnp.maximum(i - 1, 0)
    z_d = [scores(h, i) for h in heads]
    z_p = [scores(h, jp) for h in heads]
    cum_d = [suffix_sum(jnp.where(strict, log_keep(z), 0.0)) for z in z_d]
    cum_p = [suffix_sum(jnp.where(has_prev, log_keep(z), 0.0)) for z in z_p]
    carry_d = [c[:, 0:1] for c in cum_d]
    w_d = [jnp.where(strict, jnp.exp(z + c), 0.0) for z, c in zip(z_d, cum_d)]
    w_p = [jnp.where(has_prev, jnp.exp(z + c + cr), 0.0) for z, c, cr in zip(z_p, cum_p, carry_d)]
    acc = [_dot(wd.astype(_bf16), values(h, i)) + _dot(wp.astype(_bf16), values(h, jp))
           for h, wd, wp in zip(heads, w_d, w_p)]
    carry = [cd + cp[:, 0:1] for cd, cp in zip(carry_d, cum_p)]

    def cond(st):
        j, carry, _ = st
        top = functools.reduce(jnp.maximum, carry)
        return jnp.logical_and(j >= 0, jnp.max(top) > SB_EXIT)

    def body(st):
        j, carry, acc = st
        z_j = [scores(h, j) for h in heads]
        cum = [suffix_sum(log_keep(z)) for z in z_j]
        w_j = [jnp.exp(z + c + cr) for z, c, cr in zip(z_j, cum, carry)]
        acc = tuple(a + _dot(w.astype(_bf16), values(h, j)) for h, a, w in zip(heads, acc, w_j))
        carry = tuple(cr + c[:, 0:1] for cr, c in zip(carry, cum))
        return j - 1, carry, acc

    _, _, acc = lax.while_loop(cond, body, (i - 2, tuple(carry), tuple(acc)))
    for h in heads:
        zg = z_ref[h].astype(_f32)
        o_ref[:, h * LANES:(h + 1) * LANES] = (acc[h] * (zg / (1.0 + jnp.exp(-zg)))).astype(o_ref.dtype)


def _sb_attention(proj, b, s, *, t=256, hg=4):
    m = b * s
    nq = s // t
    return pl.pallas_call(
        functools.partial(_sb_kernel, t=t, hg=hg),
        grid=(b, SB_HEADS // hg, nq),
        in_specs=[
            pl.BlockSpec((hg, t, LANES), lambda bi, g, i: (G_QSB // hg + g, bi * nq + i, 0)),
            pl.BlockSpec((hg, s, LANES), lambda bi, g, i: (G_KSB // hg + g, bi, 0)),
            pl.BlockSpec((hg, s, LANES), lambda bi, g, i: (G_VSB // hg + g, bi, 0)),
            pl.BlockSpec((hg, t, LANES), lambda bi, g, i: (G_ZSB // hg + g, bi * nq + i, 0)),
        ],
        out_specs=pl.BlockSpec((t, hg * LANES), lambda bi, g, i: (bi * nq + i, g)),
        out_shape=jax.ShapeDtypeStruct((m, SB_HEADS * HEAD_DIM), _bf16),
        compiler_params=pltpu.CompilerParams(
            dimension_semantics=("arbitrary", "arbitrary", "arbitrary"),
            vmem_limit_bytes=VMEM_LIMIT),
        name="sb_attention",
    )(proj, proj, proj, proj)


def _sa_kernel(qi_ref, wq_ref, kx_ref, q_ref, k_ref, ka_ref, vt_ref, z_ref, o_ref,
               sc_ref, qsw_ref, qcat_ref, ahead_ref, kn_ref, mx_ref, *, t, rc, wide, hgrp, ahead_items, nsel):
    i = pl.program_id(1)
    nkb = i + 1
    qpos = i * t + lax.broadcasted_iota(_i32, (1, t), 1)
    vis_end = (qpos // CHUNK + 1) * CHUNK
    krow = lax.broadcasted_iota(_i32, (t, 1), 0)

    w_t = wq_ref[...].T
    w_rows = [w_t[IDX_DIM + h:IDX_DIM + h + 1, :] * (IDX_HEADS ** -0.5 * IDX_DIM ** -0.5)
              for h in range(IDX_HEADS)]
    for g in range(IDX_HEADS // 2):
        qsw_ref[g] = pltpu.roll(qi_ref[g], IDX_DIM, 1)

    def score_block(c, _):
        ks = pl.multiple_of(c * t, t)
        kxb = kx_ref[pl.ds(ks, t), :]
        acc = jnp.zeros((t, t), _f32)
        for h in range(IDX_HEADS):
            qh = qi_ref[h // 2] if h % 2 == 0 else qsw_ref[h // 2]
            acc = acc + jnp.maximum(_dot_nt(kxb, qh), 0.0) * w_rows[h]
        sc_ref[pl.ds(ks, t), :] = jnp.where(ks + krow < vis_end, acc, -jnp.inf)
        return 0

    lax.fori_loop(0, nkb, score_block, 0)

    def key_to_float(key):
        key = jnp.maximum(key, KEY_NEG_INF)
        return lax.bitcast_convert_type(key ^ ((key >> 31) & 0x7FFFFFFF), _f32)

    def count(pred):
        def inner(c, acc):
            ks = pl.multiple_of(c * t, t)
            for u in range(t // rc):
                acc = acc + jnp.where(pred(sc_ref[pl.ds(ks + u * rc, rc), :]), 1, 0)
            return acc
        acc = lax.fori_loop(0, nkb, inner, jnp.zeros((rc, t), _i32))
        return jnp.sum(acc, axis=0, keepdims=True)

    def bisect(p, st):
        key, cnt_key = st
        cand = key + lax.shift_left(jnp.int32(1), 31 - p)
        cand_f = key_to_float(cand)
        cnt = count(lambda blk: blk >= cand_f)
        ok = cnt >= nsel
        return jnp.where(ok, cand, key), jnp.where(ok, cnt, cnt_key)

    key, cnt_key = lax.fori_loop(
        0, 32, bisect, (jnp.full((1, t), INT_MIN, _i32), jnp.full((1, t), nsel, _i32)))
    thr = key_to_float(key)

    tied = jnp.logical_and(cnt_key > nsel, thr > -jnp.inf)

    @pl.when(jnp.max(jnp.where(tied, 1, 0)) > 0)
    def _():
        need = (nsel - count(lambda blk: blk > thr)).astype(_f32)
        r_i = lax.broadcasted_iota(_i32, (t, t), 0)
        c_i = lax.broadcasted_iota(_i32, (t, t), 1)
        before = (c_i < r_i).astype(_bf16)

        def fix(c, run):
            ks = pl.multiple_of(c * t, t)
            blk = sc_ref[pl.ds(ks, t), :]
            eq = blk == thr
            eqf = jnp.where(eq, 1.0, 0.0)
            rank = _dot(before, eqf.astype(_bf16)) + run
            sc_ref[pl.ds(ks, t), :] = jnp.where(jnp.logical_and(eq, rank >= need), -jnp.inf, blk)
            return run + jnp.sum(eqf, axis=0, keepdims=True)

        lax.fori_loop(0, nkb, fix, jnp.zeros((1, t), _f32))

    thr_sel = jnp.maximum(thr, F32_LOWEST)

    def to_mask(c, _):
        ks = pl.multiple_of(c * t, t)
        for u in range(t // rc):
            rows = pl.ds(ks + u * rc, rc)
            sc_ref[rows, :] = jnp.where(sc_ref[rows, :] >= thr_sel, 0.0, MASKED)
        return 0

    lax.fori_loop(0, nkb, to_mask, 0)

    wt = wide * t
    n_span = lax.shift_right_logical(i + wide, wide.bit_length() - 1)
    last = n_span - 1

    def mask_tail(c, _):
        sc_ref[pl.ds(pl.multiple_of(c * t, t), t), :] = jnp.full((t, t), MASKED, _f32)
        return 0

    lax.fori_loop(nkb, n_span * wide, mask_tail, 0)
    ahead_ref[...] = jnp.maximum(
        last * wt + lax.broadcasted_iota(_i32, (wt, 1), 0) - qpos, 0).astype(_f32)

    n_groups = SA_HEADS // hgrp

    @pl.when(i == 0)
    def _():
        rows = 2 * t
        for h in range(SA_HEADS):
            def longest(c, best):
                kk = k_ref[h, pl.ds(pl.multiple_of(c * rows, rows), rows), :].astype(_f32)
                return jnp.maximum(best, jnp.max(jnp.sum(kk * kk, axis=1, keepdims=True), axis=0, keepdims=True))
            best = lax.fori_loop(0, k_ref.shape[1] // rows, longest, jnp.zeros((1, 1), _f32))
            kn_ref[h:h + 1, :] = jnp.broadcast_to(best, (1, LANES))

    lane = lax.broadcasted_iota(_i32, (t, LANES), 1)
    tpos = i * t + lax.broadcasted_iota(_i32, (t, LANES), 0)
    a_t = (tpos >> 6).astype(_f32)
    b_t = (tpos & (CHUNK - 1)).astype(_f32)

    def set_queries(bound):
        for h in range(SA_HEADS):
            m_h = 2.0 ** -(h + 1)
            qa = jnp.where(lane == 0, CHUNK * m_h,
                           jnp.where(lane == 1, m_h,
                                     jnp.where(lane == 2, -CHUNK * m_h * a_t,
                                               jnp.where(lane == 3, -m_h * b_t, 0.0))))
            if bound is not None:
                qa = jnp.where(lane == 4, -bound(h), qa)
            qcat_ref[h] = jnp.concatenate([q_ref[h], qa.astype(_bf16)], axis=1)

    def logits(h, c, j, is_last):
        rows = pl.ds(pl.multiple_of(c * wt, wt) + j * t, t)
        lhs = jnp.concatenate([k_ref[h, rows, :], ka_ref[rows, :]], axis=1)
        lg = _dot_nt(lhs, qcat_ref[h]) + sc_ref[rows, :]
        if is_last:
            two_m = lax.bitcast_convert_type(jnp.full((1, t), 127, _i32) - h << 23, _f32)
            lg = lg - two_m * ahead_ref[j * t:(j + 1) * t, :]
        return lg

    def attend():
        def head_group(hp, l_min):
            heads = tuple(hp + a * n_groups for a in range(hgrp))

            def span(c, states, is_last):
                l_run = [st[0] for st in states]
                acc = [st[1] for st in states]
                items = [(j, a) for j in range(wide) for a in range(hgrp)]
                ready = [logits(heads[a], c, j, is_last) for j, a in items[:ahead_items]]
                for n, (j, a) in enumerate(items):
                    if n + ahead_items < len(items):
                        jn, an = items[n + ahead_items]
                        ready.append(logits(heads[an], c, jn, is_last))
                    p = jnp.exp(ready[n])
                    l_run[a] = l_run[a] + jnp.sum(p, axis=0, keepdims=True)
                    acc[a] = acc[a] + _dot(vt_ref[heads[a], c * wide + j], p.astype(_bf16))
                return tuple(zip(l_run, acc))

            init = (jnp.zeros((1, t), _f32), jnp.zeros((HEAD_DIM, t), _f32))
            states = lax.fori_loop(0, last, lambda c, st: span(c, st, False), (init,) * hgrp)
            states = span(last, states, True)
            for h, (l_run, acc) in zip(heads, states):
                out = (acc * (1.0 / l_run)).T
                zg = z_ref[h].astype(_f32)
                o_ref[h] = (out * (zg / (1.0 + jnp.exp(-zg)))).astype(o_ref.dtype)
                l_min = jnp.minimum(l_min, l_run)
            return l_min

        return jnp.min(lax.fori_loop(0, n_groups, head_group, jnp.full((1, t), jnp.inf, _f32)))

    def norm_bound(h):
        qq = q_ref[h].astype(_f32)
        return jnp.sqrt(jnp.sum(qq * qq, axis=1, keepdims=True) * kn_ref[h:h + 1, 0:1])

    set_queries(norm_bound)
    l_min = attend()

    @pl.when(jnp.logical_not(l_min > SOFTMAX_FLOOR))
    def _():
        set_queries(None)

        def head_max(h, _):
            def span_max(c, best, is_last):
                for j in range(wide):
                    best = jnp.maximum(best, jnp.max(logits(h, c, j, is_last), axis=0, keepdims=True))
                return best
            best = lax.fori_loop(0, last, lambda c, b: span_max(c, b, False), jnp.full((1, t), MASKED, _f32))
            mx_ref[pl.ds(h, 1), :] = span_max(last, best, True)
            return 0

        mx_ref[...] = jnp.zeros(mx_ref.shape, _f32)
        lax.fori_loop(0, SA_HEADS, head_max, 0)
        mx_t = mx_ref[...].T
        set_queries(lambda h: mx_t[:, h:h + 1])
        attend()


def _sa_attention(proj, misc, kx, ka, vt, b, s, *, t=256, rc=64, wide=4, hgrp=4, ahead_items=4):
    m = b * s
    nq = s // t
    nsel = min(TOPK_MAX, s // 4)
    once = pl.Buffered(1)
    return pl.pallas_call(
        functools.partial(_sa_kernel, t=t, rc=rc, wide=wide, hgrp=hgrp, ahead_items=ahead_items, nsel=nsel),
        grid=(b, nq),
        in_specs=[
            pl.BlockSpec((8, t, LANES), lambda bi, i: (G_QIX // 8, bi * nq + i, 0)),
            pl.BlockSpec((t, LANES), lambda bi, i: (bi * nq + i, 0)),
            pl.BlockSpec((s, LANES), lambda bi, i: (bi, 0), pipeline_mode=once),
            pl.BlockSpec((8, t, LANES), lambda bi, i: (G_QSA // 8, bi * nq + i, 0)),
            pl.BlockSpec((8, s, LANES), lambda bi, i: (G_KSA // 8, bi, 0), pipeline_mode=once),
            pl.BlockSpec((s, LANES), lambda bi, i: (0, 0), pipeline_mode=once),
            pl.BlockSpec((8, s // t, LANES, t), lambda bi, i: (0, bi, 0, 0), pipeline_mode=once),
            pl.BlockSpec((8, t, LANES), lambda bi, i: (G_ZSA // 8, bi * nq + i, 0)),
        ],
        out_specs=pl.BlockSpec((8, t, LANES), lambda bi, i: (0, bi * nq + i, 0)),
        out_shape=jax.ShapeDtypeStruct((SA_HEADS, m, LANES), _bf16),
        scratch_shapes=[
            pltpu.VMEM((s, t), _f32),
            pltpu.VMEM((IDX_HEADS // 2, t, LANES), _bf16),
            pltpu.VMEM((SA_HEADS, t, 2 * LANES), _bf16),
            pltpu.VMEM((wide * t, t), _f32),
            pltpu.VMEM((SA_HEADS, LANES), _f32),
            pltpu.VMEM((LANES, t), _f32),
        ],
        compiler_params=pltpu.CompilerParams(
            dimension_semantics=("arbitrary", "arbitrary"), vmem_limit_bytes=VMEM_LIMIT),
        name="sa_attention",
    )(proj, misc, kx, proj, proj, ka, vt, proj)


def _sigmoid(v):
    return 1.0 / (1.0 + jnp.exp(-v))


def _out_kernel(asb_ref, asa_ref, gsb_ref, gsa_ref, x_ref, wsb_ref, wsa_ref, wo_ref, fg_ref,
                o_ref, *, final):
    asa = jnp.concatenate([asa_ref[h] for h in range(SA_HEADS)], axis=1)
    ysb = _dot(asb_ref[...], wsb_ref[...])
    ysa = _dot(asa, wsa_ref[...])
    n_g = D_MODEL // LANES
    gsb = jnp.concatenate([gsb_ref[g] for g in range(n_g)], axis=1).astype(_f32)
    gsa = jnp.concatenate([gsa_ref[g] for g in range(n_g)], axis=1).astype(_f32)
    mixed = _sigmoid(gsb) * ysb + _sigmoid(gsa) * ysa
    hres = x_ref[...] + _dot(mixed.astype(_bf16), wo_ref[...])
    if final:
        ms = jnp.mean(hres * hres, axis=-1, keepdims=True)
        hres = (hres * lax.rsqrt(ms + RMS_EPS)) * fg_ref[...]
    o_ref[...] = hres


def _out_projection(a_sb, a_sa, proj, x2, w_sb, w_sa, w_o, fgain, *, final, tm=256):
    m, d = x2.shape
    n_g = d // LANES
    once = pl.Buffered(1)
    return pl.pallas_call(
        functools.partial(_out_kernel, final=final),
        grid=(m // tm,),
        in_specs=[
            pl.BlockSpec((tm, SB_HEADS * HEAD_DIM), lambda i: (i, 0)),
            pl.BlockSpec((SA_HEADS, tm, LANES), lambda i: (0, i, 0)),
            pl.BlockSpec((n_g, tm, LANES), lambda i: (G_GSB // n_g, i, 0)),
            pl.BlockSpec((n_g, tm, LANES), lambda i: (G_GSA // n_g, i, 0)),
            pl.BlockSpec((tm, d), lambda i: (i, 0)),
            pl.BlockSpec(w_sb.shape, lambda i: (0, 0), pipeline_mode=once),
            pl.BlockSpec(w_sa.shape, lambda i: (0, 0), pipeline_mode=once),
            pl.BlockSpec(w_o.shape, lambda i: (0, 0), pipeline_mode=once),
            pl.BlockSpec((1, d), lambda i: (0, 0)),
        ],
        out_specs=pl.BlockSpec((tm, d), lambda i: (i, 0)),
        out_shape=jax.ShapeDtypeStruct((m, d), _f32),
        compiler_params=pltpu.CompilerParams(
            dimension_semantics=("arbitrary",), vmem_limit_bytes=VMEM_LIMIT),
        name="out_projection",
    )(a_sb, a_sa, proj, proj, x2, w_sb, w_sa, w_o, fgain)


def _split_in_weights(w):
    c_qix = 4 * SB_HEADS * HEAD_DIM + 4 * SA_HEADS * HEAD_DIM
    c_kix = c_qix + IDX_HEADS * IDX_DIM
    c_gsb = c_kix + IDX_DIM + IDX_HEADS
    w_main = jnp.concatenate([w[:, :c_qix], w[:, c_gsb:], w[:, c_qix:c_kix]], axis=1)
    w_misc = jnp.pad(w[:, c_kix:c_gsb], ((0, 0), (0, LANES - (c_gsb - c_kix))))
    return w_main.astype(_bf16), w_misc.astype(_bf16)


def _alibi_key_columns(s):
    pos = jnp.arange(s, dtype=_i32)
    one = jnp.ones_like(pos)
    cols = jnp.stack([pos // CHUNK, pos % CHUNK, one, one, one], axis=1)
    return jnp.pad(cols, ((0, 0), (0, LANES - 5))).astype(_bf16)


def kernel(x, norm_gain, w_in, w_branch_sb, w_branch_sa, w_out, final_norm_gain):
    b, s, d = x.shape
    depth = norm_gain.shape[0]
    h2 = x.reshape(b * s, d)
    fgain = final_norm_gain.reshape(1, d)
    for layer in range(depth):
        w_main, w_misc = _split_in_weights(w_in[layer])
        proj, misc, vt = _in_projection(h2, norm_gain[layer].reshape(1, d), w_main, w_misc)
        kx = jnp.pad(misc[:, :IDX_DIM], ((0, 0), (0, LANES - IDX_DIM))).astype(_bf16)
        a_sb = _sb_attention(proj, b, s)
        a_sa = _sa_attention(proj, misc, kx, _alibi_key_columns(s), vt, b, s)
        h2 = _out_projection(a_sb, a_sa, proj, h2,
                             w_branch_sb[layer].astype(_bf16), w_branch_sa[layer].astype(_bf16),
                             w_out[layer].astype(_bf16), fgain, final=(layer == depth - 1))
    return h2.reshape(b, s, d)
```

```python
import functools

import jax
import jax.numpy as jnp
from jax import lax
from jax.experimental import pallas as pl
from jax.experimental.pallas import tpu as pltpu

D_MODEL = 2048
CHUNK = 64
SB_HEADS = 8
SA_HEADS = 8
HEAD_DIM = 128
IDX_HEADS = 16
IDX_DIM = 64
TOPK_MAX = 256
RMS_EPS = 1e-6

LANES = 128
VMEM_LIMIT = 60 * 1024 * 1024

G_QSB, G_KSB, G_VSB, G_ZSB = 0, 8, 16, 24
G_QSA, G_KSA, G_VSA, G_ZSA = 32, 40, 48, 56
G_GSB, G_GSA, G_QIX = 64, 80, 96
N_GROUPS = 104
GROUPS_PER_TILE = 8

INT_MIN = -2 ** 31
KEY_NEG_INF = INT_MIN + 0x7FFFFF
F32_LOWEST = -3.4028234663852886e38
MASKED = -1e30
ATTN_SCALE = HEAD_DIM ** -0.5
SB_EXIT = -104.0
SOFTMAX_FLOOR = 1e-17
FAR = 2 ** 30

_f32 = jnp.float32
_bf16 = jnp.bfloat16
_i32 = jnp.int32


def _dot_nt(a, b):
    return lax.dot_general(a, b, (((1,), (1,)), ((), ())), preferred_element_type=_f32)


def _dot(a, b):
    return jnp.dot(a, b, preferred_element_type=_f32)


def _inproj_kernel(x_ref, g_ref, w_ref, wm_ref, o_ref, misc_ref, vt_ref, u_ref, *, tm, sub, dsub):
    n = pl.program_id(1)

    @pl.when(n == 0)
    def _():
        for r in range(tm // sub):
            x = x_ref[r * sub:(r + 1) * sub, :]
            ms = jnp.mean(x * x, axis=-1, keepdims=True)
            u = (x * lax.rsqrt(ms + RMS_EPS)) * g_ref[...]
            ub = u.astype(_bf16)
            u_ref[r * sub:(r + 1) * sub, :] = ub
            misc_ref[r * sub:(r + 1) * sub, :] = _dot(ub, wm_ref[...])

    is_query = jnp.logical_or(n == G_QSB // GROUPS_PER_TILE, n == G_QSA // GROUPS_PER_TILE)
    mult = jnp.where(is_query, ATTN_SCALE, 1.0).astype(_f32)
    for r in range(tm // dsub):
        res = _dot(u_ref[r * dsub:(r + 1) * dsub, :], w_ref[...]) * mult
        for g in range(GROUPS_PER_TILE):
            o_ref[g, r * dsub:(r + 1) * dsub, :] = res[:, g * LANES:(g + 1) * LANES].astype(_bf16)

        @pl.when(n == G_VSA // GROUPS_PER_TILE)
        def _():
            for g in range(GROUPS_PER_TILE):
                for q in range(dsub // sub):
                    vt_ref[g, r * (dsub // sub) + q] = (
                        res[q * sub:(q + 1) * sub, g * LANES:(g + 1) * LANES].T.astype(_bf16))


def _in_projection(x2, gain, w_main, w_misc, *, tm=1024, sub=256, dsub=1024):
    m, d = x2.shape
    n_tiles = N_GROUPS // GROUPS_PER_TILE
    tn = GROUPS_PER_TILE * LANES
    return pl.pallas_call(
        functools.partial(_inproj_kernel, tm=tm, sub=sub, dsub=dsub),
        grid=(m // tm, n_tiles),
        in_specs=[
            pl.BlockSpec((tm, d), lambda i, j: (i, 0)),
            pl.BlockSpec((1, d), lambda i, j: (0, 0)),
            pl.BlockSpec((d, tn), lambda i, j: (0, j)),
            pl.BlockSpec((d, LANES), lambda i, j: (0, 0)),
        ],
        out_specs=[
            pl.BlockSpec((GROUPS_PER_TILE, tm, LANES), lambda i, j: (j, i, 0)),
            pl.BlockSpec((tm, LANES), lambda i, j: (i, 0)),
            pl.BlockSpec((SA_HEADS, tm // sub, LANES, sub), lambda i, j: (0, i, 0, 0)),
        ],
        out_shape=[
            jax.ShapeDtypeStruct((N_GROUPS, m, LANES), _bf16),
            jax.ShapeDtypeStruct((m, LANES), _f32),
            jax.ShapeDtypeStruct((SA_HEADS, m // sub, LANES, sub), _bf16),
        ],
        scratch_shapes=[pltpu.VMEM((tm, d), _bf16)],
        compiler_params=pltpu.CompilerParams(
            dimension_semantics=("arbitrary", "arbitrary"), vmem_limit_bytes=VMEM_LIMIT),
        name="in_projection",
    )(x2, gain, w_main, w_misc)


def _sb_kernel(q_ref, k_ref, v_ref, z_ref, o_ref, *, t, hg):
    i = pl.program_id(2)
    row = lax.broadcasted_iota(_i32, (t, t), 0)
    col = lax.broadcasted_iota(_i32, (t, t), 1)
    tri = (row >= col).astype(_bf16)
    strict = col < row
    heads = range(hg)

    def scores(h, j):
        return _dot_nt(q_ref[h], k_ref[h, pl.ds(pl.multiple_of(j * t, t), t), :])

    def log_keep(z):
        return jnp.minimum(-z, 0.0) - jnp.log(1.0 + jnp.exp(-jnp.abs(z)))

    def suffix_sum(lk):
        hi = lk.astype(_bf16)
        lo = (lk - hi.astype(_f32)).astype(_bf16)
        return _dot(hi, tri) + _dot(lo, tri)

    def values(h, j):
        return v_ref[h, pl.ds(pl.multiple_of(j * t, t), t), :]

    has_prev = i > 0
    jp = jnp.maximum(i - 1, 0)
    z_d = [scores(h, i) for h in heads]
    z_p = [scores(h, jp) for h in heads]
    cum_d = [suffix_sum(jnp.where(strict, log_keep(z), 0.0)) for z in z_d]
    cum_p = [suffix_sum(jnp.where(has_prev, log_keep(z), 0.0)) for z in z_p]
    carry_d = [c[:, 0:1] for c in cum_d]
    w_d = [jnp.where(strict, jnp.exp(z + c), 0.0) for z, c in zip(z_d, cum_d)]
    w_p = [jnp.where(has_prev, jnp.exp(z + c + cr), 0.0) for z, c, cr in zip(z_p, cum_p, carry_d)]
    acc = [_dot(wd.astype(_bf16), values(h, i)) + _dot(wp.astype(_bf16), values(h, jp))
           for h, wd, wp in zip(heads, w_d, w_p)]
    carry = [cd + cp[:, 0:1] for cd, cp in zip(carry_d, cum_p)]

    def cond(st):
        j, carry, _ = st
        top = functools.reduce(jnp.maximum, carry)
        return jnp.logical_and(j >= 0, jnp.max(top) > SB_EXIT)

    def body(st):
        j, carry, acc = st
        z_j = [scores(h, j) for h in heads]
        cum = [suffix_sum(log_keep(z)) for z in z_j]
        w_j = [jnp.exp(z + c + cr) for z, c, cr in zip(z_j, cum, carry)]
        acc = tuple(a + _dot(w.astype(_bf16), values(h, j)) for h, a, w in zip(heads, acc, w_j))
        carry = tuple(cr + c[:, 0:1] for cr, c in zip(carry, cum))
        return j - 1, carry, acc

    _, _, acc = lax.while_loop(cond, body, (i - 2, tuple(carry), tuple(acc)))
    for h in heads:
        zg = z_ref[h].astype(_f32)
        o_ref[:, h * LANES:(h + 1) * LANES] = (acc[h] * (zg / (1.0 + jnp.exp(-zg)))).astype(o_ref.dtype)


def _sb_attention(proj, b, s, *, t=256, hg=4):
    m = b * s
    nq = s // t
    return pl.pallas_call(
        functools.partial(_sb_kernel, t=t, hg=hg),
        grid=(b, SB_HEADS // hg, nq),
        in_specs=[
            pl.BlockSpec((hg, t, LANES), lambda bi, g, i: (G_QSB // hg + g, bi * nq + i, 0)),
            pl.BlockSpec((hg, s, LANES), lambda bi, g, i: (G_KSB // hg + g, bi, 0)),
            pl.BlockSpec((hg, s, LANES), lambda bi, g, i: (G_VSB // hg + g, bi, 0)),
            pl.BlockSpec((hg, t, LANES), lambda bi, g, i: (G_ZSB // hg + g, bi * nq + i, 0)),
        ],
        out_specs=pl.BlockSpec((t, hg * LANES), lambda bi, g, i: (bi * nq + i, g)),
        out_shape=jax.ShapeDtypeStruct((m, SB_HEADS * HEAD_DIM), _bf16),
        compiler_params=pltpu.CompilerParams(
            dimension_semantics=("arbitrary", "arbitrary", "arbitrary"),
            vmem_limit_bytes=VMEM_LIMIT),
        name="sb_attention",
    )(proj, proj, proj, proj)


def _sa_kernel(qi_ref, wq_ref, kx_ref, q_ref, k_ref, ka_ref, vt_ref, z_ref, o_ref,
               sc_ref, s16_ref, qsw_ref, qcat_ref, ahead_ref, kn_ref, mx_ref,
               *, t, rc, rc16, wide, hgrp, ahead_items, nsel):
    i = pl.program_id(1)
    nkb = i + 1
    qpos = i * t + lax.broadcasted_iota(_i32, (1, t), 1)
    vis_end = (qpos // CHUNK + 1) * CHUNK
    krow = lax.broadcasted_iota(_i32, (t, 1), 0)

    w_t = wq_ref[...].T
    w_rows = [w_t[IDX_DIM + h:IDX_DIM + h + 1, :] * (IDX_HEADS ** -0.5 * IDX_DIM ** -0.5)
              for h in range(IDX_HEADS)]
    for g in range(IDX_HEADS // 2):
        qsw_ref[g] = pltpu.roll(qi_ref[g], IDX_DIM, 1)

    def score_block(c, _):
        ks = pl.multiple_of(c * t, t)
        kxb = kx_ref[pl.ds(ks, t), :]
        acc = jnp.zeros((t, t), _f32)
        for h in range(IDX_HEADS):
            qh = qi_ref[h // 2] if h % 2 == 0 else qsw_ref[h // 2]
            acc = acc + jnp.maximum(_dot_nt(kxb, qh), 0.0) * w_rows[h]
        visible = jnp.where(ks + krow < vis_end, acc, -jnp.inf)
        sc_ref[pl.ds(ks, t), :] = visible
        s16_ref[pl.ds(ks, t), :] = visible.astype(_bf16)
        return 0

    lax.fori_loop(0, nkb, score_block, 0)

    def key_to_float(key):
        key = jnp.maximum(key, KEY_NEG_INF)
        return lax.bitcast_convert_type(key ^ ((key >> 31) & 0x7FFFFFFF), _f32)

    def count(pred):
        def inner(c, acc):
            ks = pl.multiple_of(c * t, t)
            for u in range(t // rc):
                acc = acc + jnp.where(pred(sc_ref[pl.ds(ks + u * rc, rc), :]), 1, 0)
            return acc
        acc = lax.fori_loop(0, nkb, inner, jnp.zeros((rc, t), _i32))
        return jnp.sum(acc, axis=0, keepdims=True)

    def key16_to_float(k16):
        k16 = jnp.maximum(k16, KEY_NEG_INF >> 16)
        return lax.bitcast_convert_type(((k16 ^ ((k16 >> 15) & 0x7FFF)) & 0xFFFF) << 16, _f32)

    def count16(cand):
        one, zero = jnp.ones((), _bf16), jnp.zeros((), _bf16)

        def inner(c, acc):
            ks = pl.multiple_of(c * t, t)
            for u in range(t // rc16):
                acc = acc + jnp.where(s16_ref[pl.ds(ks + u * rc16, rc16), :] >= cand, one, zero)
            return acc
        acc = lax.fori_loop(0, nkb, inner, jnp.zeros((rc16, t), _bf16))
        return jnp.sum(acc.astype(_f32), axis=0, keepdims=True).astype(_i32)

    def bisect16(p, k16):
        cand = k16 + lax.shift_left(jnp.int32(1), 15 - p)
        ok = count16(key16_to_float(cand).astype(_bf16)) >= nsel
        return jnp.where(ok, cand, k16)

    k16 = lax.fori_loop(0, 16, bisect16, jnp.full((1, t), -2 ** 15, _i32))
    thr16 = lax.bitcast_convert_type(key16_to_float(k16), _i32)
    low = (thr16 ^ ((thr16 >> 31) & 0x7FFFFFFF)) - (2 ** 15 + 1)

    def bisect(p, st):
        key, cnt_key = st
        cand = key + lax.shift_left(jnp.int32(1), 16 - p)
        cand_f = key_to_float(cand)
        cnt = count(lambda blk: blk >= cand_f)
        ok = cnt >= nsel
        return jnp.where(ok, cand, key), jnp.where(ok, cnt, cnt_key)

    key, cnt_key = lax.fori_loop(0, 17, bisect, (low, jnp.full((1, t), nsel, _i32)))
    thr = key_to_float(key)

    tied = jnp.logical_and(cnt_key > nsel, thr > -jnp.inf)

    @pl.when(jnp.max(jnp.where(tied, 1, 0)) > 0)
    def _():
        need = (nsel - count(lambda blk: blk > thr)).astype(_f32)
        r_i = lax.broadcasted_iota(_i32, (t, t), 0)
        c_i = lax.broadcasted_iota(_i32, (t, t), 1)
        before = (c_i < r_i).astype(_bf16)

        def fix(c, run):
            ks = pl.multiple_of(c * t, t)
            blk = sc_ref[pl.ds(ks, t), :]
            eq = blk == thr
            eqf = jnp.where(eq, 1.0, 0.0)
            rank = _dot(before, eqf.astype(_bf16)) + run
            sc_ref[pl.ds(ks, t), :] = jnp.where(jnp.logical_and(eq, rank >= need), -jnp.inf, blk)
            return run + jnp.sum(eqf, axis=0, keepdims=True)

        lax.fori_loop(0, nkb, fix, jnp.zeros((1, t), _f32))

    thr_sel = jnp.maximum(thr, F32_LOWEST)

    def to_mask(c, nearest):
        ks = pl.multiple_of(c * t, t)
        for u in range(t // rc):
            rows = pl.ds(ks + u * rc, rc)
            sel = sc_ref[rows, :] >= thr_sel
            sc_ref[rows, :] = jnp.where(sel, 0.0, MASKED)
            dist = jnp.abs(ks + u * rc + lax.broadcasted_iota(_i32, (rc, 1), 0) - qpos)
            nearest = jnp.minimum(nearest, jnp.where(sel, dist, FAR))
        return nearest

    nearest = lax.fori_loop(0, nkb, to_mask, jnp.full((rc, t), FAR, _i32))
    nearest = jnp.min(nearest, axis=0, keepdims=True).astype(_f32)

    wt = wide * t
    n_span = lax.shift_right_logical(i + wide, wide.bit_length() - 1)
    last = n_span - 1

    def mask_tail(c, _):
        sc_ref[pl.ds(pl.multiple_of(c * t, t), t), :] = jnp.full((t, t), MASKED, _f32)
        return 0

    lax.fori_loop(nkb, n_span * wide, mask_tail, 0)
    ahead_ref[...] = jnp.maximum(
        last * wt + lax.broadcasted_iota(_i32, (wt, 1), 0) - qpos, 0).astype(_f32)

    n_groups = SA_HEADS // hgrp

    @pl.when(i == 0)
    def _():
        rows = 2 * t
        for h in range(SA_HEADS):
            def longest(c, best):
                kk = k_ref[h, pl.ds(pl.multiple_of(c * rows, rows), rows), :].astype(_f32)
                return jnp.maximum(best, jnp.max(jnp.sum(kk * kk, axis=1, keepdims=True), axis=0, keepdims=True))
            best = lax.fori_loop(0, k_ref.shape[1] // rows, longest, jnp.zeros((1, 1), _f32))
            kn_ref[h:h + 1, :] = jnp.broadcast_to(best, (1, LANES))

    lane = lax.broadcasted_iota(_i32, (t, LANES), 1)
    tpos = i * t + lax.broadcasted_iota(_i32, (t, LANES), 0)
    a_t = (tpos >> 6).astype(_f32)
    b_t = (tpos & (CHUNK - 1)).astype(_f32)

    def set_queries(bound):
        for h in range(SA_HEADS):
            m_h = 2.0 ** -(h + 1)
            qa = jnp.where(lane == 0, CHUNK * m_h,
                           jnp.where(lane == 1, m_h,
                                     jnp.where(lane == 2, -CHUNK * m_h * a_t,
                                               jnp.where(lane == 3, -m_h * b_t, 0.0))))
            if bound is not None:
                qa = jnp.where(lane == 4, -bound(h), qa)
            qcat_ref[h] = jnp.concatenate([q_ref[h], qa.astype(_bf16)], axis=1)

    def logits(h, c, j, is_last):
        rows = pl.ds(pl.multiple_of(c * wt, wt) + j * t, t)
        lhs = jnp.concatenate([k_ref[h, rows, :], ka_ref[rows, :]], axis=1)
        lg = _dot_nt(lhs, qcat_ref[h]) + sc_ref[rows, :]
        if is_last:
            two_m = lax.bitcast_convert_type(jnp.full((1, t), 127, _i32) - h << 23, _f32)
            lg = lg - two_m * ahead_ref[j * t:(j + 1) * t, :]
        return lg

    def attend():
        def head_group(hp, l_min):
            heads = tuple(hp + a * n_groups for a in range(hgrp))

            def span(c, states, is_last):
                l_run = [st[0] for st in states]
                acc = [st[1] for st in states]
                items = [(j, a) for j in range(wide) for a in range(hgrp)]
                ready = [logits(heads[a], c, j, is_last) for j, a in items[:ahead_items]]
                for n, (j, a) in enumerate(items):
                    if n + ahead_items < len(items):
                        jn, an = items[n + ahead_items]
                        ready.append(logits(heads[an], c, jn, is_last))
                    p = jnp.exp(ready[n])
                    l_run[a] = l_run[a] + jnp.sum(p, axis=0, keepdims=True)
                    acc[a] = acc[a] + _dot(vt_ref[heads[a], c * wide + j], p.astype(_bf16))
                return tuple(zip(l_run, acc))

            init = (jnp.zeros((1, t), _f32), jnp.zeros((HEAD_DIM, t), _f32))
            states = lax.fori_loop(0, last, lambda c, st: span(c, st, False), (init,) * hgrp)
            states = span(last, states, True)
            for h, (l_run, acc) in zip(heads, states):
                out = (acc * (1.0 / l_run)).T
                zg = z_ref[h].astype(_f32)
                o_ref[h] = (out * (zg / (1.0 + jnp.exp(-zg)))).astype(o_ref.dtype)
                l_min = jnp.minimum(l_min, l_run)
            return l_min

        return jnp.min(lax.fori_loop(0, n_groups, head_group, jnp.full((1, t), jnp.inf, _f32)))

    mx_ref[...] = jnp.zeros(mx_ref.shape, _f32)
    mx_ref[0:1, :] = nearest
    nearest_col = mx_ref[...].T[:, 0:1]

    def logit_bound(h):
        qq = q_ref[h].astype(_f32)
        reach = jnp.sqrt(jnp.sum(qq * qq, axis=1, keepdims=True) * kn_ref[h:h + 1, 0:1])
        return reach - 2.0 ** -(h + 1) * nearest_col

    set_queries(logit_bound)
    l_min = attend()

    @pl.when(jnp.logical_not(l_min > SOFTMAX_FLOOR))
    def _():
        set_queries(None)

        def head_max(h, _):
            def span_max(c, best, is_last):
                for j in range(wide):
                    best = jnp.maximum(best, jnp.max(logits(h, c, j, is_last), axis=0, keepdims=True))
                return best
            best = lax.fori_loop(0, last, lambda c, b: span_max(c, b, False), jnp.full((1, t), MASKED, _f32))
            mx_ref[pl.ds(h, 1), :] = span_max(last, best, True)
            return 0

        mx_ref[...] = jnp.zeros(mx_ref.shape, _f32)
        lax.fori_loop(0, SA_HEADS, head_max, 0)
        mx_t = mx_ref[...].T
        set_queries(lambda h: mx_t[:, h:h + 1])
        attend()


def _sa_attention(proj, misc, kx, ka, vt, b, s, *, t=256, rc=64, rc16=128, wide=4, hgrp=4, ahead_items=4):
    m = b * s
    nq = s // t
    nsel = min(TOPK_MAX, s // 4)
    once = pl.Buffered(1)
    return pl.pallas_call(
        functools.partial(_sa_kernel, t=t, rc=rc, rc16=rc16, wide=wide, hgrp=hgrp, ahead_items=ahead_items, nsel=nsel),
        grid=(b, nq),
        in_specs=[
            pl.BlockSpec((8, t, LANES), lambda bi, i: (G_QIX // 8, bi * nq + i, 0)),
            pl.BlockSpec((t, LANES), lambda bi, i: (bi * nq + i, 0)),
            pl.BlockSpec((s, LANES), lambda bi, i: (bi, 0), pipeline_mode=once),
            pl.BlockSpec((8, t, LANES), lambda bi, i: (G_QSA // 8, bi * nq + i, 0)),
            pl.BlockSpec((8, s, LANES), lambda bi, i: (G_KSA // 8, bi, 0), pipeline_mode=once),
            pl.BlockSpec((s, LANES), lambda bi, i: (0, 0), pipeline_mode=once),
            pl.BlockSpec((8, s // t, LANES, t), lambda bi, i: (0, bi, 0, 0), pipeline_mode=once),
            pl.BlockSpec((8, t, LANES), lambda bi, i: (G_ZSA // 8, bi * nq + i, 0)),
        ],
        out_specs=pl.BlockSpec((8, t, LANES), lambda bi, i: (0, bi * nq + i, 0)),
        out_shape=jax.ShapeDtypeStruct((SA_HEADS, m, LANES), _bf16),
        scratch_shapes=[
            pltpu.VMEM((s, t), _f32),
            pltpu.VMEM((s, t), _bf16),
            pltpu.VMEM((IDX_HEADS // 2, t, LANES), _bf16),
            pltpu.VMEM((SA_HEADS, t, 2 * LANES), _bf16),
            pltpu.VMEM((wide * t, t), _f32),
            pltpu.VMEM((SA_HEADS, LANES), _f32),
            pltpu.VMEM((LANES, t), _f32),
        ],
        compiler_params=pltpu.CompilerParams(
            dimension_semantics=("arbitrary", "arbitrary"), vmem_limit_bytes=VMEM_LIMIT),
        name="sa_attention",
    )(proj, misc, kx, proj, proj, ka, vt, proj)


def _sigmoid(v):
    return 1.0 / (1.0 + jnp.exp(-v))


def _out_kernel(asb_ref, asa_ref, gsb_ref, gsa_ref, x_ref, wsb_ref, wsa_ref, wo_ref, fg_ref,
                o_ref, *, final):
    asa = jnp.concatenate([asa_ref[h] for h in range(SA_HEADS)], axis=1)
    ysb = _dot(asb_ref[...], wsb_ref[...])
    ysa = _dot(asa, wsa_ref[...])
    n_g = D_MODEL // LANES
    gsb = jnp.concatenate([gsb_ref[g] for g in range(n_g)], axis=1).astype(_f32)
    gsa = jnp.concatenate([gsa_ref[g] for g in range(n_g)], axis=1).astype(_f32)
    mixed = _sigmoid(gsb) * ysb + _sigmoid(gsa) * ysa
    hres = x_ref[...] + _dot(mixed.astype(_bf16), wo_ref[...])
    if final:
        ms = jnp.mean(hres * hres, axis=-1, keepdims=True)
        hres = (hres * lax.rsqrt(ms + RMS_EPS)) * fg_ref[...]
    o_ref[...] = hres


def _out_projection(a_sb, a_sa, proj, x2, w_sb, w_sa, w_o, fgain, *, final, tm=256):
    m, d = x2.shape
    n_g = d // LANES
    once = pl.Buffered(1)
    return pl.pallas_call(
        functools.partial(_out_kernel, final=final),
        grid=(m // tm,),
        in_specs=[
            pl.BlockSpec((tm, SB_HEADS * HEAD_DIM), lambda i: (i, 0)),
            pl.BlockSpec((SA_HEADS, tm, LANES), lambda i: (0, i, 0)),
            pl.BlockSpec((n_g, tm, LANES), lambda i: (G_GSB // n_g, i, 0)),
            pl.BlockSpec((n_g, tm, LANES), lambda i: (G_GSA // n_g, i, 0)),
            pl.BlockSpec((tm, d), lambda i: (i, 0)),
            pl.BlockSpec(w_sb.shape, lambda i: (0, 0), pipeline_mode=once),
            pl.BlockSpec(w_sa.shape, lambda i: (0, 0), pipeline_mode=once),
            pl.BlockSpec(w_o.shape, lambda i: (0, 0), pipeline_mode=once),
            pl.BlockSpec((1, d), lambda i: (0, 0)),
        ],
        out_specs=pl.BlockSpec((tm, d), lambda i: (i, 0)),
        out_shape=jax.ShapeDtypeStruct((m, d), _f32),
        compiler_params=pltpu.CompilerParams(
            dimension_semantics=("arbitrary",), vmem_limit_bytes=VMEM_LIMIT),
        name="out_projection",
    )(a_sb, a_sa, proj, proj, x2, w_sb, w_sa, w_o, fgain)


def _split_in_weights(w):
    c_qix = 4 * SB_HEADS * HEAD_DIM + 4 * SA_HEADS * HEAD_DIM
    c_kix = c_qix + IDX_HEADS * IDX_DIM
    c_gsb = c_kix + IDX_DIM + IDX_HEADS
    w_main = jnp.concatenate([w[:, :c_qix], w[:, c_gsb:], w[:, c_qix:c_kix]], axis=1)
    w_misc = jnp.pad(w[:, c_kix:c_gsb], ((0, 0), (0, LANES - (c_gsb - c_kix))))
    return w_main.astype(_bf16), w_misc.astype(_bf16)


def _alibi_key_columns(s):
    pos = jnp.arange(s, dtype=_i32)
    one = jnp.ones_like(pos)
    cols = jnp.stack([pos // CHUNK, pos % CHUNK, one, one, one], axis=1)
    return jnp.pad(cols, ((0, 0), (0, LANES - 5))).astype(_bf16)


def kernel(x, norm_gain, w_in, w_branch_sb, w_branch_sa, w_out, final_norm_gain):
    b, s, d = x.shape
    depth = norm_gain.shape[0]
    h2 = x.reshape(b * s, d)
    fgain = final_norm_gain.reshape(1, d)
    for layer in range(depth):
        w_main, w_misc = _split_in_weights(w_in[layer])
        proj, misc, vt = _in_projection(h2, norm_gain[layer].reshape(1, d), w_main, w_misc)
        kx = jnp.pad(misc[:, :IDX_DIM], ((0, 0), (0, LANES - IDX_DIM))).astype(_bf16)
        a_sb = _sb_attention(proj, b, s)
        a_sa = _sa_attention(proj, misc, kx, _alibi_key_columns(s), vt, b, s)
        h2 = _out_projection(a_sb, a_sa, proj, h2,
                             w_branch_sb[layer].astype(_bf16), w_branch_sa[layer].astype(_bf16),
                             w_out[layer].astype(_bf16), fgain, final=(layer == depth - 1))
    return h2.reshape(b, s, d)
```

```python
import functools

import jax
import jax.numpy as jnp
from jax import lax
from jax.experimental import pallas as pl
from jax.experimental.pallas import tpu as pltpu

D_MODEL = 2048
CHUNK = 64
SB_HEADS = 8
SA_HEADS = 8
HEAD_DIM = 128
IDX_HEADS = 16
IDX_DIM = 64
TOPK_MAX = 256
RMS_EPS = 1e-6

LANES = 128
VMEM_LIMIT = 60 * 1024 * 1024

G_QSB, G_KSB, G_VSB, G_ZSB = 0, 8, 16, 24
G_QSA, G_KSA, G_VSA, G_ZSA = 32, 40, 48, 56
G_GSB, G_GSA, G_QIX = 64, 80, 96
N_GROUPS = 104
GROUPS_PER_TILE = 8

INT_MIN = -2 ** 31
KEY_NEG_INF = INT_MIN + 0x7FFFFF
F32_LOWEST = -3.4028234663852886e38
MASKED = -1e30
ATTN_SCALE = HEAD_DIM ** -0.5
SB_EXIT = -104.0
SOFTMAX_FLOOR = 1e-17
FAR = 2 ** 30
STAGE2_UNCHECKED = 9

_f32 = jnp.float32
_bf16 = jnp.bfloat16
_i32 = jnp.int32


def _dot_nt(a, b):
    return lax.dot_general(a, b, (((1,), (1,)), ((), ())), preferred_element_type=_f32)


def _dot(a, b):
    return jnp.dot(a, b, preferred_element_type=_f32)


def _inproj_kernel(x_ref, g_ref, w_ref, wm_ref, o_ref, misc_ref, vt_ref, u_ref, *, tm, sub, dsub):
    n = pl.program_id(1)

    @pl.when(n == 0)
    def _():
        for r in range(tm // sub):
            x = x_ref[r * sub:(r + 1) * sub, :]
            ms = jnp.mean(x * x, axis=-1, keepdims=True)
            u = (x * lax.rsqrt(ms + RMS_EPS)) * g_ref[...]
            ub = u.astype(_bf16)
            u_ref[r * sub:(r + 1) * sub, :] = ub
            misc_ref[r * sub:(r + 1) * sub, :] = _dot(ub, wm_ref[...])

    is_query = jnp.logical_or(n == G_QSB // GROUPS_PER_TILE, n == G_QSA // GROUPS_PER_TILE)
    mult = jnp.where(is_query, ATTN_SCALE, 1.0).astype(_f32)
    for r in range(tm // dsub):
        res = _dot(u_ref[r * dsub:(r + 1) * dsub, :], w_ref[...]) * mult
        for g in range(GROUPS_PER_TILE):
            o_ref[g, r * dsub:(r + 1) * dsub, :] = res[:, g * LANES:(g + 1) * LANES].astype(_bf16)

        @pl.when(n == G_VSA // GROUPS_PER_TILE)
        def _():
            for g in range(GROUPS_PER_TILE):
                for q in range(dsub // sub):
                    vt_ref[g, r * (dsub // sub) + q] = (
                        res[q * sub:(q + 1) * sub, g * LANES:(g + 1) * LANES].T.astype(_bf16))


def _in_projection(x2, gain, w_main, w_misc, *, tm=1024, sub=256, dsub=1024):
    m, d = x2.shape
    n_tiles = N_GROUPS // GROUPS_PER_TILE
    tn = GROUPS_PER_TILE * LANES
    return pl.pallas_call(
        functools.partial(_inproj_kernel, tm=tm, sub=sub, dsub=dsub),
        grid=(m // tm, n_tiles),
        in_specs=[
            pl.BlockSpec((tm, d), lambda i, j: (i, 0)),
            pl.BlockSpec((1, d), lambda i, j: (0, 0)),
            pl.BlockSpec((d, tn), lambda i, j: (0, j)),
            pl.BlockSpec((d, LANES), lambda i, j: (0, 0)),
        ],
        out_specs=[
            pl.BlockSpec((GROUPS_PER_TILE, tm, LANES), lambda i, j: (j, i, 0)),
            pl.BlockSpec((tm, LANES), lambda i, j: (i, 0)),
            pl.BlockSpec((SA_HEADS, tm // sub, LANES, sub), lambda i, j: (0, i, 0, 0)),
        ],
        out_shape=[
            jax.ShapeDtypeStruct((N_GROUPS, m, LANES), _bf16),
            jax.ShapeDtypeStruct((m, LANES), _f32),
            jax.ShapeDtypeStruct((SA_HEADS, m // sub, LANES, sub), _bf16),
        ],
        scratch_shapes=[pltpu.VMEM((tm, d), _bf16)],
        compiler_params=pltpu.CompilerParams(
            dimension_semantics=("arbitrary", "arbitrary"), vmem_limit_bytes=VMEM_LIMIT),
        name="in_projection",
    )(x2, gain, w_main, w_misc)


def _sb_kernel(q_ref, k_ref, v_ref, z_ref, o_ref, *, t, hg):
    i = pl.program_id(2)
    row = lax.broadcasted_iota(_i32, (t, t), 0)
    col = lax.broadcasted_iota(_i32, (t, t), 1)
    tri = (row >= col).astype(_bf16)
    strict = col < row
    heads = range(hg)

    def scores(h, j):
        return _dot_nt(q_ref[h], k_ref[h, pl.ds(pl.multiple_of(j * t, t), t), :])

    def log_keep(z):
        return jnp.minimum(-z, 0.0) - jnp.log(1.0 + jnp.exp(-jnp.abs(z)))

    def suffix_sum(lk):
        hi = lk.astype(_bf16)
        lo = (lk - hi.astype(_f32)).astype(_bf16)
        return _dot(hi, tri) + _dot(lo, tri)

    def values(h, j):
        return v_ref[h, pl.ds(pl.multiple_of(j * t, t), t), :]

    has_prev = i > 0
    jp = jnp.maximum(i - 1, 0)
    z_d = [scores(h, i) for h in heads]
    z_p = [scores(h, jp) for h in heads]
    cum_d = [suffix_sum(jnp.where(strict, log_keep(z), 0.0)) for z in z_d]
    cum_p = [suffix_sum(jnp.where(has_prev, log_keep(z), 0.0)) for z in z_p]
    carry_d = [c[:, 0:1] for c in cum_d]
    w_d = [jnp.where(strict, jnp.exp(z + c), 0.0) for z, c in zip(z_d, cum_d)]
    w_p = [jnp.where(has_prev, jnp.exp(z + c + cr), 0.0) for z, c, cr in zip(z_p, cum_p, carry_d)]
    acc = [_dot(wd.astype(_bf16), values(h, i)) + _dot(wp.astype(_bf16), values(h, jp))
           for h, wd, wp in zip(heads, w_d, w_p)]
    carry = [cd + cp[:, 0:1] for cd, cp in zip(carry_d, cum_p)]

    def cond(st):
        j, carry, _ = st
        top = functools.reduce(jnp.maximum, carry)
        return jnp.logical_and(j >= 0, jnp.max(top) > SB_EXIT)

    def body(st):
        j, carry, acc = st
        z_j = [scores(h, j) for h in heads]
        cum = [suffix_sum(log_keep(z)) for z in z_j]
        w_j = [jnp.exp(z + c + cr) for z, c, cr in zip(z_j, cum, carry)]
        acc = tuple(a + _dot(w.astype(_bf16), values(h, j)) for h, a, w in zip(heads, acc, w_j))
        carry = tuple(cr + c[:, 0:1] for cr, c in zip(carry, cum))
        return j - 1, carry, acc

    _, _, acc = lax.while_loop(cond, body, (i - 2, tuple(carry), tuple(acc)))
    for h in heads:
        zg = z_ref[h].astype(_f32)
        o_ref[:, h * LANES:(h + 1) * LANES] = (acc[h] * (zg / (1.0 + jnp.exp(-zg)))).astype(o_ref.dtype)


def _sb_attention(proj, b, s, *, t=256, hg=4):
    m = b * s
    nq = s // t
    return pl.pallas_call(
        functools.partial(_sb_kernel, t=t, hg=hg),
        grid=(b, SB_HEADS // hg, nq),
        in_specs=[
            pl.BlockSpec((hg, t, LANES), lambda bi, g, i: (G_QSB // hg + g, bi * nq + i, 0)),
            pl.BlockSpec((hg, s, LANES), lambda bi, g, i: (G_KSB // hg + g, bi, 0)),
            pl.BlockSpec((hg, s, LANES), lambda bi, g, i: (G_VSB // hg + g, bi, 0)),
            pl.BlockSpec((hg, t, LANES), lambda bi, g, i: (G_ZSB // hg + g, bi * nq + i, 0)),
        ],
        out_specs=pl.BlockSpec((t, hg * LANES), lambda bi, g, i: (bi * nq + i, g)),
        out_shape=jax.ShapeDtypeStruct((m, SB_HEADS * HEAD_DIM), _bf16),
        compiler_params=pltpu.CompilerParams(
            dimension_semantics=("arbitrary", "arbitrary", "arbitrary"),
            vmem_limit_bytes=VMEM_LIMIT),
        name="sb_attention",
    )(proj, proj, proj, proj)


def _sa_kernel(qi_ref, wq_ref, kx_ref, q_ref, k_ref, ka_ref, vt_ref, z_ref, o_ref,
               sc_ref, s16_ref, qsw_ref, qcat_ref, ahead_ref, kn_ref, mx_ref,
               *, t, rc, rc16, sblk, wide, hgrp, ahead_items, nsel):
    i = pl.program_id(1)
    nkb = i + 1
    qpos = i * t + lax.broadcasted_iota(_i32, (1, t), 1)
    vis_end = (qpos // CHUNK + 1) * CHUNK
    krow = lax.broadcasted_iota(_i32, (t, 1), 0)

    w_t = wq_ref[...].T
    w_rows = [w_t[IDX_DIM + h:IDX_DIM + h + 1, :] * (IDX_HEADS ** -0.5 * IDX_DIM ** -0.5)
              for h in range(IDX_HEADS)]
    for g in range(IDX_HEADS // 2):
        qsw_ref[g] = pltpu.roll(qi_ref[g], IDX_DIM, 1)

    st = sblk * t
    srow = lax.broadcasted_iota(_i32, (st, 1), 0)

    def score_block(c, _):
        ks = pl.multiple_of(c * st, st)
        kxb = kx_ref[pl.ds(ks, st), :]
        acc = jnp.zeros((st, t), _f32)
        for h in range(IDX_HEADS):
            qh = qi_ref[h // 2] if h % 2 == 0 else qsw_ref[h // 2]
            acc = acc + jnp.maximum(_dot_nt(kxb, qh), 0.0) * w_rows[h]
        visible = jnp.where(ks + srow < vis_end, acc, -jnp.inf)
        sc_ref[pl.ds(ks, st), :] = visible
        s16_ref[pl.ds(ks, st), :] = visible.astype(_bf16)
        return 0

    lax.fori_loop(0, (nkb + sblk - 1) // sblk, score_block, 0)

    def key_to_float(key):
        key = jnp.maximum(key, KEY_NEG_INF)
        return lax.bitcast_convert_type(key ^ ((key >> 31) & 0x7FFFFFFF), _f32)

    def count(pred):
        def inner(c, acc):
            ks = pl.multiple_of(c * t, t)
            for u in range(t // rc):
                acc = acc + jnp.where(pred(sc_ref[pl.ds(ks + u * rc, rc), :]), 1, 0)
            return acc
        acc = lax.fori_loop(0, nkb, inner, jnp.zeros((rc, t), _i32))
        return jnp.sum(acc, axis=0, keepdims=True)

    def key16_to_float(k16):
        k16 = jnp.maximum(k16, KEY_NEG_INF >> 16)
        return lax.bitcast_convert_type(((k16 ^ ((k16 >> 15) & 0x7FFF)) & 0xFFFF) << 16, _f32)

    def count16(cand):
        one, zero = jnp.ones((), _bf16), jnp.zeros((), _bf16)

        def inner(c, acc):
            ks = pl.multiple_of(c * t, t)
            for u in range(t // rc16):
                acc = acc + jnp.where(s16_ref[pl.ds(ks + u * rc16, rc16), :] >= cand, one, zero)
            return acc
        acc = lax.fori_loop(0, nkb, inner, jnp.zeros((rc16, t), _bf16))
        return jnp.sum(acc.astype(_f32), axis=0, keepdims=True).astype(_i32)

    def bisect16(p, k16):
        cand = k16 + lax.shift_left(jnp.int32(1), 15 - p)
        ok = count16(key16_to_float(cand).astype(_bf16)) >= nsel
        return jnp.where(ok, cand, k16)

    k16 = lax.fori_loop(0, 16, bisect16, jnp.full((1, t), -2 ** 15, _i32))
    thr16 = lax.bitcast_convert_type(key16_to_float(k16), _i32)
    low = (thr16 ^ ((thr16 >> 31) & 0x7FFFFFFF)) - (2 ** 15 + 1)

    def bisect(p, st):
        key, cnt_key = st
        cand = key + lax.shift_left(jnp.int32(1), 16 - p)
        cand_f = key_to_float(cand)
        cnt = count(lambda blk: blk >= cand_f)
        ok = cnt >= nsel
        return jnp.where(ok, cand, key), jnp.where(ok, cnt, cnt_key)

    def pending(cnt_key):
        return jnp.max(jnp.where(cnt_key == nsel, 0, 1))

    def refine(st):
        p, _, key, cnt_key = st
        key, cnt_key = bisect(p, (key, cnt_key))
        return p + 1, pending(cnt_key), key, cnt_key

    key, cnt_key = lax.fori_loop(0, STAGE2_UNCHECKED, bisect, (low, jnp.zeros((1, t), _i32)))
    _, _, key, cnt_key = lax.while_loop(
        lambda st: jnp.logical_and(st[0] < 17, st[1] > 0), refine,
        (jnp.int32(STAGE2_UNCHECKED), pending(cnt_key), key, cnt_key))
    thr = key_to_float(key)

    tied = jnp.logical_and(cnt_key > nsel, thr > -jnp.inf)

    @pl.when(jnp.max(jnp.where(tied, 1, 0)) > 0)
    def _():
        need = (nsel - count(lambda blk: blk > thr)).astype(_f32)
        r_i = lax.broadcasted_iota(_i32, (t, t), 0)
        c_i = lax.broadcasted_iota(_i32, (t, t), 1)
        before = (c_i < r_i).astype(_bf16)

        def fix(c, run):
            ks = pl.multiple_of(c * t, t)
            blk = sc_ref[pl.ds(ks, t), :]
            eq = blk == thr
            eqf = jnp.where(eq, 1.0, 0.0)
            rank = _dot(before, eqf.astype(_bf16)) + run
            sc_ref[pl.ds(ks, t), :] = jnp.where(jnp.logical_and(eq, rank >= need), -jnp.inf, blk)
            return run + jnp.sum(eqf, axis=0, keepdims=True)

        lax.fori_loop(0, nkb, fix, jnp.zeros((1, t), _f32))

    thr_sel = jnp.maximum(thr, F32_LOWEST)

    def to_mask(c, nearest):
        ks = pl.multiple_of(c * t, t)
        for u in range(t // rc):
            rows = pl.ds(ks + u * rc, rc)
            sel = sc_ref[rows, :] >= thr_sel
            sc_ref[rows, :] = jnp.where(sel, 0.0, MASKED)
            dist = jnp.abs(ks + u * rc + lax.broadcasted_iota(_i32, (rc, 1), 0) - qpos)
            nearest = jnp.minimum(nearest, jnp.where(sel, dist, FAR))
        return nearest

    nearest = lax.fori_loop(0, nkb, to_mask, jnp.full((rc, t), FAR, _i32))
    nearest = jnp.min(nearest, axis=0, keepdims=True).astype(_f32)

    wt = wide * t
    n_span = lax.shift_right_logical(i + wide, wide.bit_length() - 1)
    last = n_span - 1

    def mask_tail(c, _):
        sc_ref[pl.ds(pl.multiple_of(c * t, t), t), :] = jnp.full((t, t), MASKED, _f32)
        return 0

    lax.fori_loop(nkb, n_span * wide, mask_tail, 0)
    ahead_ref[...] = jnp.maximum(
        last * wt + lax.broadcasted_iota(_i32, (wt, 1), 0) - qpos, 0).astype(_f32)

    n_groups = SA_HEADS // hgrp

    @pl.when(i == 0)
    def _():
        rows = 2 * t
        for h in range(SA_HEADS):
            def longest(c, best):
                kk = k_ref[h, pl.ds(pl.multiple_of(c * rows, rows), rows), :].astype(_f32)
                return jnp.maximum(best, jnp.max(jnp.sum(kk * kk, axis=1, keepdims=True), axis=0, keepdims=True))
            best = lax.fori_loop(0, k_ref.shape[1] // rows, longest, jnp.zeros((1, 1), _f32))
            kn_ref[h:h + 1, :] = jnp.broadcast_to(best, (1, LANES))

    lane = lax.broadcasted_iota(_i32, (t, LANES), 1)
    tpos = i * t + lax.broadcasted_iota(_i32, (t, LANES), 0)
    a_t = (tpos >> 6).astype(_f32)
    b_t = (tpos & (CHUNK - 1)).astype(_f32)

    def set_queries(bound):
        for h in range(SA_HEADS):
            m_h = 2.0 ** -(h + 1)
            qa = jnp.where(lane == 0, CHUNK * m_h,
                           jnp.where(lane == 1, m_h,
                                     jnp.where(lane == 2, -CHUNK * m_h * a_t,
                                               jnp.where(lane == 3, -m_h * b_t, 0.0))))
            if bound is not None:
                qa = jnp.where(lane == 4, -bound(h), qa)
            qcat_ref[h] = jnp.concatenate([q_ref[h], qa.astype(_bf16)], axis=1)

    def logits(h, c, j, is_last):
        rows = pl.ds(pl.multiple_of(c * wt, wt) + j * t, t)
        lhs = jnp.concatenate([k_ref[h, rows, :], ka_ref[rows, :]], axis=1)
        lg = _dot_nt(lhs, qcat_ref[h]) + sc_ref[rows, :]
        if is_last:
            two_m = lax.bitcast_convert_type(jnp.full((1, t), 127, _i32) - h << 23, _f32)
            lg = lg - two_m * ahead_ref[j * t:(j + 1) * t, :]
        return lg

    def attend():
        def head_group(hp, l_min):
            heads = tuple(hp + a * n_groups for a in range(hgrp))

            def span(c, states, is_last):
                l_run = [st[0] for st in states]
                acc = [st[1] for st in states]
                items = [(j, a) for j in range(wide) for a in range(hgrp)]
                ready = [logits(heads[a], c, j, is_last) for j, a in items[:ahead_items]]
                for n, (j, a) in enumerate(items):
                    if n + ahead_items < len(items):
                        jn, an = items[n + ahead_items]
                        ready.append(logits(heads[an], c, jn, is_last))
                    p = jnp.exp(ready[n])
                    l_run[a] = l_run[a] + jnp.sum(p, axis=0, keepdims=True)
                    acc[a] = acc[a] + _dot(vt_ref[heads[a], c * wide + j], p.astype(_bf16))
                return tuple(zip(l_run, acc))

            init = (jnp.zeros((1, t), _f32), jnp.zeros((HEAD_DIM, t), _f32))
            states = lax.fori_loop(0, last, lambda c, st: span(c, st, False), (init,) * hgrp)
            states = span(last, states, True)
            for h, (l_run, acc) in zip(heads, states):
                out = (acc * (1.0 / l_run)).T
                zg = z_ref[h].astype(_f32)
                o_ref[h] = (out * (zg / (1.0 + jnp.exp(-zg)))).astype(o_ref.dtype)
                l_min = jnp.minimum(l_min, l_run)
            return l_min

        return jnp.min(lax.fori_loop(0, n_groups, head_group, jnp.full((1, t), jnp.inf, _f32)))

    mx_ref[...] = jnp.zeros(mx_ref.shape, _f32)
    mx_ref[0:1, :] = nearest
    nearest_col = mx_ref[...].T[:, 0:1]

    def logit_bound(h):
        qq = q_ref[h].astype(_f32)
        reach = jnp.sqrt(jnp.sum(qq * qq, axis=1, keepdims=True) * kn_ref[h:h + 1, 0:1])
        return reach - 2.0 ** -(h + 1) * nearest_col

    set_queries(logit_bound)
    l_min = attend()

    @pl.when(jnp.logical_not(l_min > SOFTMAX_FLOOR))
    def _():
        set_queries(None)

        def head_max(h, _):
            def span_max(c, best, is_last):
                for j in range(wide):
                    best = jnp.maximum(best, jnp.max(logits(h, c, j, is_last), axis=0, keepdims=True))
                return best
            best = lax.fori_loop(0, last, lambda c, b: span_max(c, b, False), jnp.full((1, t), MASKED, _f32))
            mx_ref[pl.ds(h, 1), :] = span_max(last, best, True)
            return 0

        mx_ref[...] = jnp.zeros(mx_ref.shape, _f32)
        lax.fori_loop(0, SA_HEADS, head_max, 0)
        mx_t = mx_ref[...].T
        set_queries(lambda h: mx_t[:, h:h + 1])
        attend()


def _sa_attention(proj, misc, kx, ka, vt, b, s, *, t=256, rc=64, rc16=128, sblk=2, wide=4, hgrp=4, ahead_items=4):
    m = b * s
    nq = s // t
    nsel = min(TOPK_MAX, s // 4)
    once = pl.Buffered(1)
    return pl.pallas_call(
        functools.partial(_sa_kernel, t=t, rc=rc, rc16=rc16, sblk=sblk, wide=wide, hgrp=hgrp, ahead_items=ahead_items, nsel=nsel),
        grid=(b, nq),
        in_specs=[
            pl.BlockSpec((8, t, LANES), lambda bi, i: (G_QIX // 8, bi * nq + i, 0)),
            pl.BlockSpec((t, LANES), lambda bi, i: (bi * nq + i, 0)),
            pl.BlockSpec((s, LANES), lambda bi, i: (bi, 0), pipeline_mode=once),
            pl.BlockSpec((8, t, LANES), lambda bi, i: (G_QSA // 8, bi * nq + i, 0)),
            pl.BlockSpec((8, s, LANES), lambda bi, i: (G_KSA // 8, bi, 0), pipeline_mode=once),
            pl.BlockSpec((s, LANES), lambda bi, i: (0, 0), pipeline_mode=once),
            pl.BlockSpec((8, s // t, LANES, t), lambda bi, i: (0, bi, 0, 0), pipeline_mode=once),
            pl.BlockSpec((8, t, LANES), lambda bi, i: (G_ZSA // 8, bi * nq + i, 0)),
        ],
        out_specs=pl.BlockSpec((8, t, LANES), lambda bi, i: (0, bi * nq + i, 0)),
        out_shape=jax.ShapeDtypeStruct((SA_HEADS, m, LANES), _bf16),
        scratch_shapes=[
            pltpu.VMEM((s, t), _f32),
            pltpu.VMEM((s, t), _bf16),
            pltpu.VMEM((IDX_HEADS // 2, t, LANES), _bf16),
            pltpu.VMEM((SA_HEADS, t, 2 * LANES), _bf16),
            pltpu.VMEM((wide * t, t), _f32),
            pltpu.VMEM((SA_HEADS, LANES), _f32),
            pltpu.VMEM((LANES, t), _f32),
        ],
        compiler_params=pltpu.CompilerParams(
            dimension_semantics=("arbitrary", "arbitrary"), vmem_limit_bytes=VMEM_LIMIT),
        name="sa_attention",
    )(proj, misc, kx, proj, proj, ka, vt, proj)


def _sigmoid(v):
    return 1.0 / (1.0 + jnp.exp(-v))


def _out_kernel(asb_ref, asa_ref, gsb_ref, gsa_ref, x_ref, wsb_ref, wsa_ref, wo_ref, fg_ref,
                o_ref, *, final):
    asa = jnp.concatenate([asa_ref[h] for h in range(SA_HEADS)], axis=1)
    ysb = _dot(asb_ref[...], wsb_ref[...])
    ysa = _dot(asa, wsa_ref[...])
    n_g = D_MODEL // LANES
    gsb = jnp.concatenate([gsb_ref[g] for g in range(n_g)], axis=1).astype(_f32)
    gsa = jnp.concatenate([gsa_ref[g] for g in range(n_g)], axis=1).astype(_f32)
    mixed = _sigmoid(gsb) * ysb + _sigmoid(gsa) * ysa
    hres = x_ref[...] + _dot(mixed.astype(_bf16), wo_ref[...])
    if final:
        ms = jnp.mean(hres * hres, axis=-1, keepdims=True)
        hres = (hres * lax.rsqrt(ms + RMS_EPS)) * fg_ref[...]
    o_ref[...] = hres


def _out_projection(a_sb, a_sa, proj, x2, w_sb, w_sa, w_o, fgain, *, final, tm=512):
    m, d = x2.shape
    n_g = d // LANES
    once = pl.Buffered(1)
    return pl.pallas_call(
        functools.partial(_out_kernel, final=final),
        grid=(m // tm,),
        in_specs=[
            pl.BlockSpec((tm, SB_HEADS * HEAD_DIM), lambda i: (i, 0)),
            pl.BlockSpec((SA_HEADS, tm, LANES), lambda i: (0, i, 0)),
            pl.BlockSpec((n_g, tm, LANES), lambda i: (G_GSB // n_g, i, 0)),
            pl.BlockSpec((n_g, tm, LANES), lambda i: (G_GSA // n_g, i, 0)),
            pl.BlockSpec((tm, d), lambda i: (i, 0)),
            pl.BlockSpec(w_sb.shape, lambda i: (0, 0), pipeline_mode=once),
            pl.BlockSpec(w_sa.shape, lambda i: (0, 0), pipeline_mode=once),
            pl.BlockSpec(w_o.shape, lambda i: (0, 0), pipeline_mode=once),
            pl.BlockSpec((1, d), lambda i: (0, 0)),
        ],
        out_specs=pl.BlockSpec((tm, d), lambda i: (i, 0)),
        out_shape=jax.ShapeDtypeStruct((m, d), _f32),
        compiler_params=pltpu.CompilerParams(
            dimension_semantics=("arbitrary",), vmem_limit_bytes=VMEM_LIMIT),
        name="out_projection",
    )(a_sb, a_sa, proj, proj, x2, w_sb, w_sa, w_o, fgain)


def _split_in_weights(w):
    c_qix = 4 * SB_HEADS * HEAD_DIM + 4 * SA_HEADS * HEAD_DIM
    c_kix = c_qix + IDX_HEADS * IDX_DIM
    c_gsb = c_kix + IDX_DIM + IDX_HEADS
    w = w.astype(_bf16)
    w_main = jnp.concatenate([w[:, :c_qix], w[:, c_gsb:], w[:, c_qix:c_kix]], axis=1)
    w_misc = jnp.pad(w[:, c_kix:c_gsb], ((0, 0), (0, LANES - (c_gsb - c_kix))))
    return w_main, w_misc


def _alibi_key_columns(s):
    pos = jnp.arange(s, dtype=_i32)
    one = jnp.ones_like(pos)
    cols = jnp.stack([pos // CHUNK, pos % CHUNK, one, one, one], axis=1)
    return jnp.pad(cols, ((0, 0), (0, LANES - 5))).astype(_bf16)


def kernel(x, norm_gain, w_in, w_branch_sb, w_branch_sa, w_out, final_norm_gain):
    b, s, d = x.shape
    depth = norm_gain.shape[0]
    h2 = x.reshape(b * s, d)
    fgain = final_norm_gain.reshape(1, d)
    for layer in range(depth):
        w_main, w_misc = _split_in_weights(w_in[layer])
        proj, misc, vt = _in_projection(h2, norm_gain[layer].reshape(1, d), w_main, w_misc)
        kx = jnp.pad(misc[:, :IDX_DIM], ((0, 0), (0, LANES - IDX_DIM))).astype(_bf16)
        a_sb = _sb_attention(proj, b, s)
        a_sa = _sa_attention(proj, misc, kx, _alibi_key_columns(s), vt, b, s)
        h2 = _out_projection(a_sb, a_sa, proj, h2,
                             w_branch_sb[layer].astype(_bf16), w_branch_sa[layer].astype(_bf16),
                             w_out[layer].astype(_bf16), fgain, final=(layer == depth - 1))
    return h2.reshape(b, s, d)
```

```python
import functools

import jax
import jax.numpy as jnp
from jax import lax
from jax.experimental import pallas as pl
from jax.experimental.pallas import tpu as pltpu

D_MODEL = 2048
CHUNK = 64
SB_HEADS = 8
SA_HEADS = 8
HEAD_DIM = 128
IDX_HEADS = 16
IDX_DIM = 64
TOPK_MAX = 256
RMS_EPS = 1e-6

LANES = 128
VMEM_LIMIT = 60 * 1024 * 1024

G_QSB, G_KSB, G_VSB, G_ZSB = 0, 8, 16, 24
G_QSA, G_KSA, G_VSA, G_ZSA = 32, 40, 48, 56
G_GSB, G_GSA, G_QIX = 64, 80, 96
N_GROUPS = 104
GROUPS_PER_TILE = 8

INT_MIN = -2 ** 31
KEY_NEG_INF = INT_MIN + 0x7FFFFF
F32_LOWEST = -3.4028234663852886e38
MASKED = -1e30
ATTN_SCALE = HEAD_DIM ** -0.5
SB_EXIT = -104.0
SOFTMAX_FLOOR = 1e-17
FAR = 2 ** 30
STAGE2_UNCHECKED = 9

_f32 = jnp.float32
_bf16 = jnp.bfloat16
_i32 = jnp.int32


def _dot_nt(a, b):
    return lax.dot_general(a, b, (((1,), (1,)), ((), ())), preferred_element_type=_f32)


def _dot(a, b):
    return jnp.dot(a, b, preferred_element_type=_f32)


def _inproj_kernel(x_ref, g_ref, wl_ref, wg_ref, wm_ref, o_ref, misc_ref, vt_ref, u_ref,
                   *, tm, sub, dsub):
    n = pl.program_id(1)
    first_gate = G_GSB // GROUPS_PER_TILE
    is_gate = jnp.logical_and(n >= first_gate, n < G_QIX // GROUPS_PER_TILE)

    @pl.when(n == 0)
    def _():
        for r in range(tm // sub):
            x = x_ref[r * sub:(r + 1) * sub, :]
            ms = jnp.mean(x * x, axis=-1, keepdims=True)
            u = (x * lax.rsqrt(ms + RMS_EPS)) * g_ref[...]
            ub = u.astype(_bf16)
            u_ref[r * sub:(r + 1) * sub, :] = ub
            misc_ref[r * sub:(r + 1) * sub, :] = _dot(ub, wm_ref[...])

    is_query = jnp.logical_or(n == G_QSB // GROUPS_PER_TILE, n == G_QSA // GROUPS_PER_TILE)
    mult = jnp.where(is_query, ATTN_SCALE, 1.0).astype(_f32)

    def project(w_ref, with_vt):
        for r in range(tm // dsub):
            res = _dot(u_ref[r * dsub:(r + 1) * dsub, :], w_ref[...]) * mult
            for g in range(GROUPS_PER_TILE):
                o_ref[g, r * dsub:(r + 1) * dsub, :] = res[:, g * LANES:(g + 1) * LANES].astype(_bf16)
            if with_vt:
                @pl.when(n == G_VSA // GROUPS_PER_TILE)
                def _():
                    for g in range(GROUPS_PER_TILE):
                        for q in range(dsub // sub):
                            vt_ref[g, r * (dsub // sub) + q] = (
                                res[q * sub:(q + 1) * sub, g * LANES:(g + 1) * LANES].T.astype(_bf16))

    @pl.when(is_gate)
    def _():
        project(wg_ref, False)

    @pl.when(jnp.logical_not(is_gate))
    def _():
        project(wl_ref, True)


def _in_projection(x2, gain, w_lead, w_gate, w_misc, *, tm=1024, sub=256, dsub=1024):
    m, d = x2.shape
    n_tiles = N_GROUPS // GROUPS_PER_TILE
    tn = GROUPS_PER_TILE * LANES
    first_gate, n_gate = G_GSB // GROUPS_PER_TILE, w_gate.shape[1] // tn
    return pl.pallas_call(
        functools.partial(_inproj_kernel, tm=tm, sub=sub, dsub=dsub),
        grid=(m // tm, n_tiles),
        in_specs=[
            pl.BlockSpec((tm, d), lambda i, j: (i, 0)),
            pl.BlockSpec((1, d), lambda i, j: (0, 0)),
            pl.BlockSpec((d, tn), lambda i, j: (0, jnp.minimum(j, first_gate))),
            pl.BlockSpec((d, tn), lambda i, j: (0, jnp.clip(j - first_gate, 0, n_gate - 1))),
            pl.BlockSpec((d, LANES), lambda i, j: (0, 0)),
        ],
        out_specs=[
            pl.BlockSpec((GROUPS_PER_TILE, tm, LANES), lambda i, j: (j, i, 0)),
            pl.BlockSpec((tm, LANES), lambda i, j: (i, 0)),
            pl.BlockSpec((SA_HEADS, tm // sub, LANES, sub), lambda i, j: (0, i, 0, 0)),
        ],
        out_shape=[
            jax.ShapeDtypeStruct((N_GROUPS, m, LANES), _bf16),
            jax.ShapeDtypeStruct((m, LANES), _f32),
            jax.ShapeDtypeStruct((SA_HEADS, m // sub, LANES, sub), _bf16),
        ],
        scratch_shapes=[pltpu.VMEM((tm, d), _bf16)],
        compiler_params=pltpu.CompilerParams(
            dimension_semantics=("arbitrary", "arbitrary"), vmem_limit_bytes=VMEM_LIMIT),
        name="in_projection",
    )(x2, gain, w_lead, w_gate, w_misc)


def _sb_kernel(q_ref, k_ref, v_ref, z_ref, o_ref, *, t, hg):
    i = pl.program_id(2)
    row = lax.broadcasted_iota(_i32, (t, t), 0)
    col = lax.broadcasted_iota(_i32, (t, t), 1)
    tri = (row >= col).astype(_bf16)
    strict = col < row
    heads = range(hg)

    def scores(h, j):
        return _dot_nt(q_ref[h], k_ref[h, pl.ds(pl.multiple_of(j * t, t), t), :])

    def log_keep(z):
        return jnp.minimum(-z, 0.0) - jnp.log(1.0 + jnp.exp(-jnp.abs(z)))

    def suffix_sum(lk):
        hi = lk.astype(_bf16)
        lo = (lk - hi.astype(_f32)).astype(_bf16)
        return _dot(hi, tri) + _dot(lo, tri)

    def values(h, j):
        return v_ref[h, pl.ds(pl.multiple_of(j * t, t), t), :]

    has_prev = i > 0
    jp = jnp.maximum(i - 1, 0)
    z_d = [scores(h, i) for h in heads]
    z_p = [scores(h, jp) for h in heads]
    cum_d = [suffix_sum(jnp.where(strict, log_keep(z), 0.0)) for z in z_d]
    cum_p = [suffix_sum(jnp.where(has_prev, log_keep(z), 0.0)) for z in z_p]
    carry_d = [c[:, 0:1] for c in cum_d]
    w_d = [jnp.where(strict, jnp.exp(z + c), 0.0) for z, c in zip(z_d, cum_d)]
    w_p = [jnp.where(has_prev, jnp.exp(z + c + cr), 0.0) for z, c, cr in zip(z_p, cum_p, carry_d)]
    acc = [_dot(wd.astype(_bf16), values(h, i)) + _dot(wp.astype(_bf16), values(h, jp))
           for h, wd, wp in zip(heads, w_d, w_p)]
    carry = [cd + cp[:, 0:1] for cd, cp in zip(carry_d, cum_p)]

    def cond(st):
        j, carry, _ = st
        top = functools.reduce(jnp.maximum, carry)
        return jnp.logical_and(j >= 0, jnp.max(top) > SB_EXIT)

    def body(st):
        j, carry, acc = st
        z_j = [scores(h, j) for h in heads]
        cum = [suffix_sum(log_keep(z)) for z in z_j]
        w_j = [jnp.exp(z + c + cr) for z, c, cr in zip(z_j, cum, carry)]
        acc = tuple(a + _dot(w.astype(_bf16), values(h, j)) for h, a, w in zip(heads, acc, w_j))
        carry = tuple(cr + c[:, 0:1] for cr, c in zip(carry, cum))
        return j - 1, carry, acc

    _, _, acc = lax.while_loop(cond, body, (i - 2, tuple(carry), tuple(acc)))
    for h in heads:
        zg = z_ref[h].astype(_f32)
        o_ref[:, h * LANES:(h + 1) * LANES] = (acc[h] * (zg / (1.0 + jnp.exp(-zg)))).astype(o_ref.dtype)


def _sb_attention(proj, b, s, *, t=256, hg=4):
    m = b * s
    nq = s // t
    return pl.pallas_call(
        functools.partial(_sb_kernel, t=t, hg=hg),
        grid=(b, SB_HEADS // hg, nq),
        in_specs=[
            pl.BlockSpec((hg, t, LANES), lambda bi, g, i: (G_QSB // hg + g, bi * nq + i, 0)),
            pl.BlockSpec((hg, s, LANES), lambda bi, g, i: (G_KSB // hg + g, bi, 0)),
            pl.BlockSpec((hg, s, LANES), lambda bi, g, i: (G_VSB // hg + g, bi, 0)),
            pl.BlockSpec((hg, t, LANES), lambda bi, g, i: (G_ZSB // hg + g, bi * nq + i, 0)),
        ],
        out_specs=pl.BlockSpec((t, hg * LANES), lambda bi, g, i: (bi * nq + i, g)),
        out_shape=jax.ShapeDtypeStruct((m, SB_HEADS * HEAD_DIM), _bf16),
        compiler_params=pltpu.CompilerParams(
            dimension_semantics=("arbitrary", "arbitrary", "arbitrary"),
            vmem_limit_bytes=VMEM_LIMIT),
        name="sb_attention",
    )(proj, proj, proj, proj)


def _sa_kernel(qi_ref, wq_ref, kx_ref, q_ref, k_ref, ka_ref, vt_ref, z_ref, o_ref,
               sc_ref, s16_ref, qit_ref, qcat_ref, ahead_ref, kn_ref, mx_ref,
               *, t, rc, rc16, sblk, wide, hgrp, ahead_items, nsel):
    i = pl.program_id(1)
    nkb = i + 1
    qpos = i * t + lax.broadcasted_iota(_i32, (1, t), 1)
    vis_end = (qpos // CHUNK + 1) * CHUNK
    krow = lax.broadcasted_iota(_i32, (t, 1), 0)

    w_t = wq_ref[...].T
    w_rows = [w_t[IDX_DIM + h:IDX_DIM + h + 1, :] * (IDX_HEADS ** -0.5 * IDX_DIM ** -0.5)
              for h in range(IDX_HEADS)]
    for g in range(IDX_HEADS // 2):
        q_pair = qi_ref[g].astype(_f32).T
        qit_ref[2 * g] = q_pair.astype(_bf16)
        qit_ref[2 * g + 1] = jnp.concatenate([q_pair[IDX_DIM:], q_pair[:IDX_DIM]], axis=0).astype(_bf16)

    st = sblk * t
    srow = lax.broadcasted_iota(_i32, (st, 1), 0)

    def score_block(c, _):
        ks = pl.multiple_of(c * st, st)
        kxb = kx_ref[pl.ds(ks, st), :]
        acc = jnp.zeros((st, t), _f32)
        for h in range(IDX_HEADS):
            acc = acc + jnp.maximum(_dot(kxb, qit_ref[h]), 0.0) * w_rows[h]
        visible = jnp.where(ks + srow < vis_end, acc, -jnp.inf)
        sc_ref[pl.ds(ks, st), :] = visible
        s16_ref[pl.ds(ks, st), :] = visible.astype(_bf16)
        return 0

    lax.fori_loop(0, (nkb + sblk - 1) // sblk, score_block, 0)

    def key_to_float(key):
        key = jnp.maximum(key, KEY_NEG_INF)
        return lax.bitcast_convert_type(key ^ ((key >> 31) & 0x7FFFFFFF), _f32)

    def count(pred):
        def inner(c, acc):
            ks = pl.multiple_of(c * t, t)
            for u in range(t // rc):
                acc = acc + jnp.where(pred(sc_ref[pl.ds(ks + u * rc, rc), :]), 1, 0)
            return acc
        acc = lax.fori_loop(0, nkb, inner, jnp.zeros((rc, t), _i32))
        return jnp.sum(acc, axis=0, keepdims=True)

    def key16_to_float(k16):
        k16 = jnp.maximum(k16, KEY_NEG_INF >> 16)
        return lax.bitcast_convert_type(((k16 ^ ((k16 >> 15) & 0x7FFF)) & 0xFFFF) << 16, _f32)

    def count16(cand):
        one, zero = jnp.ones((), _bf16), jnp.zeros((), _bf16)

        def inner(c, acc):
            ks = pl.multiple_of(c * t, t)
            for u in range(t // rc16):
                acc = acc + jnp.where(s16_ref[pl.ds(ks + u * rc16, rc16), :] >= cand, one, zero)
            return acc
        acc = lax.fori_loop(0, nkb, inner, jnp.zeros((rc16, t), _bf16))
        return jnp.sum(acc.astype(_f32), axis=0, keepdims=True).astype(_i32)

    def bisect16(p, k16):
        cand = k16 + lax.shift_left(jnp.int32(1), 15 - p)
        ok = count16(key16_to_float(cand).astype(_bf16)) >= nsel
        return jnp.where(ok, cand, k16)

    k16 = lax.fori_loop(0, 16, bisect16, jnp.full((1, t), -2 ** 15, _i32))
    thr16 = lax.bitcast_convert_type(key16_to_float(k16), _i32)
    low = (thr16 ^ ((thr16 >> 31) & 0x7FFFFFFF)) - (2 ** 15 + 1)

    def bisect(p, st):
        key, cnt_key = st
        cand = key + lax.shift_left(jnp.int32(1), 16 - p)
        cand_f = key_to_float(cand)
        cnt = count(lambda blk: blk >= cand_f)
        ok = cnt >= nsel
        return jnp.where(ok, cand, key), jnp.where(ok, cnt, cnt_key)

    def pending(cnt_key):
        return jnp.max(jnp.where(cnt_key == nsel, 0, 1))

    def refine(st):
        p, _, key, cnt_key = st
        key, cnt_key = bisect(p, (key, cnt_key))
        return p + 1, pending(cnt_key), key, cnt_key

    key, cnt_key = lax.fori_loop(0, STAGE2_UNCHECKED, bisect, (low, jnp.zeros((1, t), _i32)))
    _, _, key, cnt_key = lax.while_loop(
        lambda st: jnp.logical_and(st[0] < 17, st[1] > 0), refine,
        (jnp.int32(STAGE2_UNCHECKED), pending(cnt_key), key, cnt_key))
    thr = key_to_float(key)

    tied = jnp.logical_and(cnt_key > nsel, thr > -jnp.inf)
    mx_ref[0:1, :] = jnp.maximum(thr, F32_LOWEST)

    @pl.when(jnp.max(jnp.where(tied, 1, 0)) > 0)
    def _():
        thr_up = key_to_float(key + 1)

        def relabel(c, _):
            ks = pl.multiple_of(c * t, t)
            for u in range(t // rc):
                rows = pl.ds(ks + u * rc, rc)
                x = sc_ref[rows, :]
                inside = jnp.logical_and(x >= thr, x > -jnp.inf)
                sc_ref[rows, :] = jnp.where(x >= thr_up, jnp.inf, jnp.where(inside, x - thr, -jnp.inf))
            return 0

        lax.fori_loop(0, nkb, relabel, 0)

        def bisect32(p, st):
            key_r, cnt_r = st
            cand = key_r + lax.shift_left(jnp.int32(1), 31 - p)
            cand_f = key_to_float(cand)
            cnt = count(lambda blk: blk >= cand_f)
            ok = cnt >= nsel
            return jnp.where(ok, cand, key_r), jnp.where(ok, cnt, cnt_r)

        key_r, _ = lax.fori_loop(
            0, 32, bisect32, (jnp.full((1, t), INT_MIN, _i32), jnp.full((1, t), nsel, _i32)))
        thr_r = key_to_float(key_r)
        need = (nsel - count(lambda blk: blk > thr_r)).astype(_f32)
        r_i = lax.broadcasted_iota(_i32, (t, t), 0)
        c_i = lax.broadcasted_iota(_i32, (t, t), 1)
        before = (c_i < r_i).astype(_bf16)

        def fix(c, run):
            ks = pl.multiple_of(c * t, t)
            blk = sc_ref[pl.ds(ks, t), :]
            eq = blk == thr_r
            eqf = jnp.where(eq, 1.0, 0.0)
            rank = _dot(before, eqf.astype(_bf16)) + run
            sc_ref[pl.ds(ks, t), :] = jnp.where(jnp.logical_and(eq, rank >= need), -jnp.inf, blk)
            return run + jnp.sum(eqf, axis=0, keepdims=True)

        lax.fori_loop(0, nkb, fix, jnp.zeros((1, t), _f32))
        mx_ref[0:1, :] = jnp.maximum(thr_r, F32_LOWEST)

    thr_sel = mx_ref[0:1, :]

    def to_mask(c, nearest):
        ks = pl.multiple_of(c * t, t)
        for u in range(t // rc):
            rows = pl.ds(ks + u * rc, rc)
            sel = sc_ref[rows, :] >= thr_sel
            sc_ref[rows, :] = jnp.where(sel, 0.0, MASKED)
            dist = jnp.abs(ks + u * rc + lax.broadcasted_iota(_i32, (rc, 1), 0) - qpos)
            nearest = jnp.minimum(nearest, jnp.where(sel, dist, FAR))
        return nearest

    nearest = lax.fori_loop(0, nkb, to_mask, jnp.full((rc, t), FAR, _i32))
    nearest = jnp.min(nearest, axis=0, keepdims=True).astype(_f32)

    wt = wide * t
    n_span = lax.shift_right_logical(i + wide, wide.bit_length() - 1)
    last = n_span - 1

    def mask_tail(c, _):
        sc_ref[pl.ds(pl.multiple_of(c * t, t), t), :] = jnp.full((t, t), MASKED, _f32)
        return 0

    lax.fori_loop(nkb, n_span * wide, mask_tail, 0)
    ahead_ref[...] = jnp.maximum(
        last * wt + lax.broadcasted_iota(_i32, (wt, 1), 0) - qpos, 0).astype(_f32)

    n_groups = SA_HEADS // hgrp

    @pl.when(i == 0)
    def _():
        rows = 2 * t
        for h in range(SA_HEADS):
            def longest(c, best):
                kk = k_ref[h, pl.ds(pl.multiple_of(c * rows, rows), rows), :].astype(_f32)
                return jnp.maximum(best, jnp.max(jnp.sum(kk * kk, axis=1, keepdims=True), axis=0, keepdims=True))
            best = lax.fori_loop(0, k_ref.shape[1] // rows, longest, jnp.zeros((1, 1), _f32))
            kn_ref[h:h + 1, :] = jnp.broadcast_to(best, (1, LANES))

    feat = lax.broadcasted_iota(_i32, (LANES, t), 0)
    a_t = (qpos >> 6).astype(_f32)
    b_t = (qpos & (CHUNK - 1)).astype(_f32)

    def set_queries(bound):
        for h in range(SA_HEADS):
            m_h = 2.0 ** -(h + 1)
            q_t = q_ref[h].astype(_f32).T
            extra = jnp.where(feat == 0, CHUNK * m_h,
                              jnp.where(feat == 1, m_h,
                                        jnp.where(feat == 2, -CHUNK * m_h * a_t,
                                                  jnp.where(feat == 3, -m_h * b_t, 0.0))))
            if bound is not None:
                extra = jnp.where(feat == 4, -bound(h, q_t), extra)
            qcat_ref[h] = jnp.concatenate([q_t, extra], axis=0).astype(_bf16)

    def logits(h, c, j, is_last):
        rows = pl.ds(pl.multiple_of(c * wt, wt) + j * t, t)
        lhs = jnp.concatenate([k_ref[h, rows, :], ka_ref[rows, :]], axis=1)
        lg = _dot(lhs, qcat_ref[h]) + sc_ref[rows, :]
        if is_last:
            two_m = lax.bitcast_convert_type(jnp.full((1, t), 127, _i32) - h << 23, _f32)
            lg = lg - two_m * ahead_ref[j * t:(j + 1) * t, :]
        return lg

    def attend():
        def head_group(hp, l_min):
            heads = tuple(hp + a * n_groups for a in range(hgrp))

            def span(c, states, is_last):
                l_run = [st[0] for st in states]
                acc = [st[1] for st in states]
                items = [(j, a) for j in range(wide) for a in range(hgrp)]
                ready = [logits(heads[a], c, j, is_last) for j, a in items[:ahead_items]]
                for n, (j, a) in enumerate(items):
                    if n + ahead_items < len(items):
                        jn, an = items[n + ahead_items]
                        ready.append(logits(heads[an], c, jn, is_last))
                    p = jnp.exp(ready[n])
                    l_run[a] = l_run[a] + jnp.sum(p, axis=0, keepdims=True)
                    acc[a] = acc[a] + _dot(vt_ref[heads[a], c * wide + j], p.astype(_bf16))
                return tuple(zip(l_run, acc))

            init = (jnp.zeros((1, t), _f32), jnp.zeros((HEAD_DIM, t), _f32))
            states = lax.fori_loop(0, last, lambda c, st: span(c, st, False), (init,) * hgrp)
            states = span(last, states, True)
            for h, (l_run, acc) in zip(heads, states):
                out = (acc * (1.0 / l_run)).T
                zg = z_ref[h].astype(_f32)
                o_ref[h] = (out * (zg / (1.0 + jnp.exp(-zg)))).astype(o_ref.dtype)
                l_min = jnp.minimum(l_min, l_run)
            return l_min

        return jnp.min(lax.fori_loop(0, n_groups, head_group, jnp.full((1, t), jnp.inf, _f32)))

    def logit_bound(h, q_t):
        reach = jnp.sqrt(jnp.sum(q_t * q_t, axis=0, keepdims=True) * kn_ref[h:h + 1, 0:1])
        return reach - 2.0 ** -(h + 1) * nearest

    set_queries(logit_bound)
    l_min = attend()

    @pl.when(jnp.logical_not(l_min > SOFTMAX_FLOOR))
    def _():
        set_queries(None)

        def head_max(h, _):
            def span_max(c, best, is_last):
                for j in range(wide):
                    best = jnp.maximum(best, jnp.max(logits(h, c, j, is_last), axis=0, keepdims=True))
                return best
            best = lax.fori_loop(0, last, lambda c, b: span_max(c, b, False), jnp.full((1, t), MASKED, _f32))
            mx_ref[pl.ds(h, 1), :] = span_max(last, best, True)
            return 0

        lax.fori_loop(0, SA_HEADS, head_max, 0)
        set_queries(lambda h, q_t: mx_ref[h:h + 1, :])
        attend()


def _sa_attention(proj, misc, kx, ka, vt, b, s, *, t=256, rc=64, rc16=128, sblk=2, wide=4, hgrp=4, ahead_items=4):
    m = b * s
    nq = s // t
    nsel = min(TOPK_MAX, s // 4)
    once = pl.Buffered(1)
    return pl.pallas_call(
        functools.partial(_sa_kernel, t=t, rc=rc, rc16=rc16, sblk=sblk, wide=wide, hgrp=hgrp, ahead_items=ahead_items, nsel=nsel),
        grid=(b, nq),
        in_specs=[
            pl.BlockSpec((8, t, LANES), lambda bi, i: (G_QIX // 8, bi * nq + i, 0)),
            pl.BlockSpec((t, LANES), lambda bi, i: (bi * nq + i, 0)),
            pl.BlockSpec((s, LANES), lambda bi, i: (bi, 0), pipeline_mode=once),
            pl.BlockSpec((8, t, LANES), lambda bi, i: (G_QSA // 8, bi * nq + i, 0)),
            pl.BlockSpec((8, s, LANES), lambda bi, i: (G_KSA // 8, bi, 0), pipeline_mode=once),
            pl.BlockSpec((s, LANES), lambda bi, i: (0, 0), pipeline_mode=once),
            pl.BlockSpec((8, s // t, LANES, t), lambda bi, i: (0, bi, 0, 0), pipeline_mode=once),
            pl.BlockSpec((8, t, LANES), lambda bi, i: (G_ZSA // 8, bi * nq + i, 0)),
        ],
        out_specs=pl.BlockSpec((8, t, LANES), lambda bi, i: (0, bi * nq + i, 0)),
        out_shape=jax.ShapeDtypeStruct((SA_HEADS, m, LANES), _bf16),
        scratch_shapes=[
            pltpu.VMEM((s, t), _f32),
            pltpu.VMEM((s, t), _bf16),
            pltpu.VMEM((IDX_HEADS, LANES, t), _bf16),
            pltpu.VMEM((SA_HEADS, 2 * LANES, t), _bf16),
            pltpu.VMEM((wide * t, t), _f32),
            pltpu.VMEM((SA_HEADS, LANES), _f32),
            pltpu.VMEM((SA_HEADS, t), _f32),
        ],
        compiler_params=pltpu.CompilerParams(
            dimension_semantics=("arbitrary", "arbitrary"), vmem_limit_bytes=VMEM_LIMIT),
        name="sa_attention",
    )(proj, misc, kx, proj, proj, ka, vt, proj)


def _sigmoid(v):
    return 1.0 / (1.0 + jnp.exp(-v))


def _out_kernel(asb_ref, asa_ref, gsb_ref, gsa_ref, x_ref, wsb_ref, wsa_ref, wo_ref, fg_ref,
                o_ref, *, final):
    asa = jnp.concatenate([asa_ref[h] for h in range(SA_HEADS)], axis=1)
    ysb = _dot(asb_ref[...], wsb_ref[...])
    ysa = _dot(asa, wsa_ref[...])
    n_g = D_MODEL // LANES
    gsb = jnp.concatenate([gsb_ref[g] for g in range(n_g)], axis=1).astype(_f32)
    gsa = jnp.concatenate([gsa_ref[g] for g in range(n_g)], axis=1).astype(_f32)
    mixed = _sigmoid(gsb) * ysb + _sigmoid(gsa) * ysa
    hres = x_ref[...] + _dot(mixed.astype(_bf16), wo_ref[...])
    if final:
        ms = jnp.mean(hres * hres, axis=-1, keepdims=True)
        hres = (hres * lax.rsqrt(ms + RMS_EPS)) * fg_ref[...]
    o_ref[...] = hres


def _out_projection(a_sb, a_sa, proj, x2, w_sb, w_sa, w_o, fgain, *, final, tm=512):
    m, d = x2.shape
    n_g = d // LANES
    once = pl.Buffered(1)
    return pl.pallas_call(
        functools.partial(_out_kernel, final=final),
        grid=(m // tm,),
        in_specs=[
            pl.BlockSpec((tm, SB_HEADS * HEAD_DIM), lambda i: (i, 0)),
            pl.BlockSpec((SA_HEADS, tm, LANES), lambda i: (0, i, 0)),
            pl.BlockSpec((n_g, tm, LANES), lambda i: (G_GSB // n_g, i, 0)),
            pl.BlockSpec((n_g, tm, LANES), lambda i: (G_GSA // n_g, i, 0)),
            pl.BlockSpec((tm, d), lambda i: (i, 0)),
            pl.BlockSpec(w_sb.shape, lambda i: (0, 0), pipeline_mode=once),
            pl.BlockSpec(w_sa.shape, lambda i: (0, 0), pipeline_mode=once),
            pl.BlockSpec(w_o.shape, lambda i: (0, 0), pipeline_mode=once),
            pl.BlockSpec((1, d), lambda i: (0, 0)),
        ],
        out_specs=pl.BlockSpec((tm, d), lambda i: (i, 0)),
        out_shape=jax.ShapeDtypeStruct((m, d), _f32),
        compiler_params=pltpu.CompilerParams(
            dimension_semantics=("arbitrary",), vmem_limit_bytes=VMEM_LIMIT),
        name="out_projection",
    )(a_sb, a_sa, proj, proj, x2, w_sb, w_sa, w_o, fgain)


def _split_in_weights(w):
    c_qix = 4 * SB_HEADS * HEAD_DIM + 4 * SA_HEADS * HEAD_DIM
    c_kix = c_qix + IDX_HEADS * IDX_DIM
    c_gsb = c_kix + IDX_DIM + IDX_HEADS
    w_lead = w[:, :c_kix].astype(_bf16)
    w_gate = w[:, c_gsb:].astype(_bf16)
    w_misc = jnp.pad(w[:, c_kix:c_gsb], ((0, 0), (0, LANES - (c_gsb - c_kix)))).astype(_bf16)
    return w_lead, w_gate, w_misc


def _alibi_key_columns(s):
    pos = jnp.arange(s, dtype=_i32)
    one = jnp.ones_like(pos)
    cols = jnp.stack([pos // CHUNK, pos % CHUNK, one, one, one], axis=1)
    return jnp.pad(cols, ((0, 0), (0, LANES - 5))).astype(_bf16)


def kernel(x, norm_gain, w_in, w_branch_sb, w_branch_sa, w_out, final_norm_gain):
    b, s, d = x.shape
    depth = norm_gain.shape[0]
    h2 = x.reshape(b * s, d)
    fgain = final_norm_gain.reshape(1, d)
    for layer in range(depth):
        w_lead, w_gate, w_misc = _split_in_weights(w_in[layer])
        proj, misc, vt = _in_projection(h2, norm_gain[layer].reshape(1, d), w_lead, w_gate, w_misc)
        kx = jnp.pad(misc[:, :IDX_DIM], ((0, 0), (0, LANES - IDX_DIM))).astype(_bf16)
        a_sb = _sb_attention(proj, b, s)
        a_sa = _sa_attention(proj, misc, kx, _alibi_key_columns(s), vt, b, s)
        h2 = _out_projection(a_sb, a_sa, proj, h2,
                             w_branch_sb[layer].astype(_bf16), w_branch_sa[layer].astype(_bf16),
                             w_out[layer].astype(_bf16), fgain, final=(layer == depth - 1))
    return h2.reshape(b, s, d)
```

```python
import functools

import jax
import jax.numpy as jnp
from jax import lax
from jax.experimental import pallas as pl
from jax.experimental.pallas import tpu as pltpu

D_MODEL = 2048
CHUNK = 64
SB_HEADS = 8
SA_HEADS = 8
HEAD_DIM = 128
IDX_HEADS = 16
IDX_DIM = 64
TOPK_MAX = 256
RMS_EPS = 1e-6

LANES = 128
VMEM_LIMIT = 60 * 1024 * 1024

G_QSB, G_KSB, G_VSB, G_ZSB = 0, 8, 16, 24
G_QSA, G_KSA, G_VSA, G_ZSA = 32, 40, 48, 56
G_GSB, G_GSA, G_QIX = 64, 80, 96
N_GROUPS = 104
GROUPS_PER_TILE = 8

INT_MIN = -2 ** 31
KEY_NEG_INF = INT_MIN + 0x7FFFFF
F32_LOWEST = -3.4028234663852886e38
MASKED = -1e30
ATTN_SCALE = HEAD_DIM ** -0.5
SB_EXIT = -104.0
SOFTMAX_FLOOR = 1e-17
FAR = 2 ** 30
STAGE2_UNCHECKED = 9

_f32 = jnp.float32
_bf16 = jnp.bfloat16
_i32 = jnp.int32


def _dot_nt(a, b):
    return lax.dot_general(a, b, (((1,), (1,)), ((), ())), preferred_element_type=_f32)


def _dot(a, b):
    return jnp.dot(a, b, preferred_element_type=_f32)


def _inproj_kernel(x_ref, g_ref, wl_ref, wg_ref, wm_ref, o_ref, misc_ref, vt_ref, u_ref,
                   *, tm, sub, dsub):
    n = pl.program_id(1)
    first_gate = G_GSB // GROUPS_PER_TILE
    is_gate = jnp.logical_and(n >= first_gate, n < G_QIX // GROUPS_PER_TILE)

    @pl.when(n == 0)
    def _():
        for r in range(tm // sub):
            x = x_ref[r * sub:(r + 1) * sub, :]
            ms = jnp.mean(x * x, axis=-1, keepdims=True)
            u = (x * lax.rsqrt(ms + RMS_EPS)) * g_ref[...]
            ub = u.astype(_bf16)
            u_ref[r * sub:(r + 1) * sub, :] = ub
            misc_ref[r * sub:(r + 1) * sub, :] = _dot(ub, wm_ref[...])

    is_query = jnp.logical_or(n == G_QSB // GROUPS_PER_TILE, n == G_QSA // GROUPS_PER_TILE)
    mult = jnp.where(is_query, ATTN_SCALE, 1.0).astype(_f32)

    def project(w_ref, with_vt):
        for r in range(tm // dsub):
            res = _dot(u_ref[r * dsub:(r + 1) * dsub, :], w_ref[...]) * mult
            for g in range(GROUPS_PER_TILE):
                o_ref[g, r * dsub:(r + 1) * dsub, :] = res[:, g * LANES:(g + 1) * LANES].astype(_bf16)
            if with_vt:
                @pl.when(n == G_VSA // GROUPS_PER_TILE)
                def _():
                    for g in range(GROUPS_PER_TILE):
                        for q in range(dsub // sub):
                            vt_ref[g, r * (dsub // sub) + q] = (
                                res[q * sub:(q + 1) * sub, g * LANES:(g + 1) * LANES].T.astype(_bf16))

    @pl.when(is_gate)
    def _():
        project(wg_ref, False)

    @pl.when(jnp.logical_not(is_gate))
    def _():
        project(wl_ref, True)


def _in_projection(x2, gain, w_lead, w_gate, w_misc, *, tm=1024, sub=256, dsub=1024):
    m, d = x2.shape
    n_tiles = N_GROUPS // GROUPS_PER_TILE
    tn = GROUPS_PER_TILE * LANES
    first_gate, n_gate = G_GSB // GROUPS_PER_TILE, w_gate.shape[1] // tn
    return pl.pallas_call(
        functools.partial(_inproj_kernel, tm=tm, sub=sub, dsub=dsub),
        grid=(m // tm, n_tiles),
        in_specs=[
            pl.BlockSpec((tm, d), lambda i, j: (i, 0)),
            pl.BlockSpec((1, d), lambda i, j: (0, 0)),
            pl.BlockSpec((d, tn), lambda i, j: (0, jnp.minimum(j, first_gate))),
            pl.BlockSpec((d, tn), lambda i, j: (0, jnp.clip(j - first_gate, 0, n_gate - 1))),
            pl.BlockSpec((d, LANES), lambda i, j: (0, 0)),
        ],
        out_specs=[
            pl.BlockSpec((GROUPS_PER_TILE, tm, LANES), lambda i, j: (j, i, 0)),
            pl.BlockSpec((tm, LANES), lambda i, j: (i, 0)),
            pl.BlockSpec((SA_HEADS, tm // sub, LANES, sub), lambda i, j: (0, i, 0, 0)),
        ],
        out_shape=[
            jax.ShapeDtypeStruct((N_GROUPS, m, LANES), _bf16),
            jax.ShapeDtypeStruct((m, LANES), _f32),
            jax.ShapeDtypeStruct((SA_HEADS, m // sub, LANES, sub), _bf16),
        ],
        scratch_shapes=[pltpu.VMEM((tm, d), _bf16)],
        compiler_params=pltpu.CompilerParams(
            dimension_semantics=("arbitrary", "arbitrary"), vmem_limit_bytes=VMEM_LIMIT),
        name="in_projection",
    )(x2, gain, w_lead, w_gate, w_misc)


def _sb_kernel(q_ref, k_ref, v_ref, z_ref, o_ref, *, t, hg):
    i = pl.program_id(2)
    row = lax.broadcasted_iota(_i32, (t, t), 0)
    col = lax.broadcasted_iota(_i32, (t, t), 1)
    tri = (row >= col).astype(_bf16)
    strict = col < row
    heads = range(hg)

    def scores(h, j):
        return _dot_nt(q_ref[h], k_ref[h, pl.ds(pl.multiple_of(j * t, t), t), :])

    def log_keep(z):
        return jnp.minimum(-z, 0.0) - jnp.log(1.0 + jnp.exp(-jnp.abs(z)))

    def suffix_sum(lk):
        hi = lk.astype(_bf16)
        lo = (lk - hi.astype(_f32)).astype(_bf16)
        return _dot(hi, tri) + _dot(lo, tri)

    def values(h, j):
        return v_ref[h, pl.ds(pl.multiple_of(j * t, t), t), :]

    has_prev = i > 0
    jp = jnp.maximum(i - 1, 0)
    z_d = [scores(h, i) for h in heads]
    z_p = [scores(h, jp) for h in heads]
    cum_d = [suffix_sum(jnp.where(strict, log_keep(z), 0.0)) for z in z_d]
    cum_p = [suffix_sum(jnp.where(has_prev, log_keep(z), 0.0)) for z in z_p]
    carry_d = [c[:, 0:1] for c in cum_d]
    w_d = [jnp.where(strict, jnp.exp(z + c), 0.0) for z, c in zip(z_d, cum_d)]
    w_p = [jnp.where(has_prev, jnp.exp(z + c + cr), 0.0) for z, c, cr in zip(z_p, cum_p, carry_d)]
    acc = [_dot(wd.astype(_bf16), values(h, i)) + _dot(wp.astype(_bf16), values(h, jp))
           for h, wd, wp in zip(heads, w_d, w_p)]
    carry = [cd + cp[:, 0:1] for cd, cp in zip(carry_d, cum_p)]

    def cond(st):
        j, carry, _ = st
        top = functools.reduce(jnp.maximum, carry)
        return jnp.logical_and(j >= 0, jnp.max(top) > SB_EXIT)

    def body(st):
        j, carry, acc = st
        z_j = [scores(h, j) for h in heads]
        cum = [suffix_sum(log_keep(z)) for z in z_j]
        w_j = [jnp.exp(z + c + cr) for z, c, cr in zip(z_j, cum, carry)]
        acc = tuple(a + _dot(w.astype(_bf16), values(h, j)) for h, a, w in zip(heads, acc, w_j))
        carry = tuple(cr + c[:, 0:1] for cr, c in zip(carry, cum))
        return j - 1, carry, acc

    _, _, acc = lax.while_loop(cond, body, (i - 2, tuple(carry), tuple(acc)))
    for h in heads:
        zg = z_ref[h].astype(_f32)
        o_ref[:, h * LANES:(h + 1) * LANES] = (acc[h] * (zg / (1.0 + jnp.exp(-zg)))).astype(o_ref.dtype)


def _sb_attention(proj, b, s, *, t=256, hg=4):
    m = b * s
    nq = s // t
    return pl.pallas_call(
        functools.partial(_sb_kernel, t=t, hg=hg),
        grid=(b, SB_HEADS // hg, nq),
        in_specs=[
            pl.BlockSpec((hg, t, LANES), lambda bi, g, i: (G_QSB // hg + g, bi * nq + i, 0)),
            pl.BlockSpec((hg, s, LANES), lambda bi, g, i: (G_KSB // hg + g, bi, 0)),
            pl.BlockSpec((hg, s, LANES), lambda bi, g, i: (G_VSB // hg + g, bi, 0)),
            pl.BlockSpec((hg, t, LANES), lambda bi, g, i: (G_ZSB // hg + g, bi * nq + i, 0)),
        ],
        out_specs=pl.BlockSpec((t, hg * LANES), lambda bi, g, i: (bi * nq + i, g)),
        out_shape=jax.ShapeDtypeStruct((m, SB_HEADS * HEAD_DIM), _bf16),
        compiler_params=pltpu.CompilerParams(
            dimension_semantics=("arbitrary", "arbitrary", "arbitrary"),
            vmem_limit_bytes=VMEM_LIMIT),
        name="sb_attention",
    )(proj, proj, proj, proj)


def _sa_kernel(qi_ref, wq_ref, kx_ref, q_ref, k_ref, ka_ref, vt_ref, z_ref, o_ref,
               sc_ref, s16_ref, qit_ref, qcat_ref, ahead_ref, kn_ref, mx_ref,
               *, t, rc, rc16, sblk, wide, hgrp, ahead_items, nsel):
    i = pl.program_id(1)
    nkb = i + 1
    qpos = i * t + lax.broadcasted_iota(_i32, (1, t), 1)
    vis_end = (qpos // CHUNK + 1) * CHUNK
    krow = lax.broadcasted_iota(_i32, (t, 1), 0)

    w_t = wq_ref[...].T
    w_rows = [w_t[IDX_DIM + h:IDX_DIM + h + 1, :] * (IDX_HEADS ** -0.5 * IDX_DIM ** -0.5)
              for h in range(IDX_HEADS)]
    for g in range(IDX_HEADS // 2):
        q_pair = qi_ref[g].astype(_f32).T
        qit_ref[2 * g] = q_pair.astype(_bf16)
        qit_ref[2 * g + 1] = jnp.concatenate([q_pair[IDX_DIM:], q_pair[:IDX_DIM]], axis=0).astype(_bf16)

    st = sblk * t
    srow = lax.broadcasted_iota(_i32, (st, 1), 0)

    def score_block(c, _):
        ks = pl.multiple_of(c * st, st)
        kxb = kx_ref[pl.ds(ks, st), :]
        acc = jnp.zeros((st, t), _f32)
        for h in range(IDX_HEADS):
            acc = acc + jnp.maximum(_dot(kxb, qit_ref[h]), 0.0) * w_rows[h]
        visible = jnp.where(ks + srow < vis_end, acc, -jnp.inf)
        sc_ref[pl.ds(ks, st), :] = visible
        s16_ref[pl.ds(ks, st), :] = visible.astype(_bf16)
        return 0

    lax.fori_loop(0, (nkb + sblk - 1) // sblk, score_block, 0)

    def key_to_float(key):
        key = jnp.maximum(key, KEY_NEG_INF)
        return lax.bitcast_convert_type(key ^ ((key >> 31) & 0x7FFFFFFF), _f32)

    def over_blocks(step, init):
        pairs = lax.shift_right_logical(nkb, 1)
        acc = lax.fori_loop(0, pairs, lambda c, acc: step(2 * c + 1, step(2 * c, acc)), init)
        return lax.fori_loop(2 * pairs, nkb, step, acc)

    def count(pred):
        def step(c, acc):
            ks = pl.multiple_of(c * t, t)
            for u in range(t // rc):
                acc = acc + jnp.where(pred(sc_ref[pl.ds(ks + u * rc, rc), :]), 1, 0)
            return acc
        return jnp.sum(over_blocks(step, jnp.zeros((rc, t), _i32)), axis=0, keepdims=True)

    def key16_to_float(k16):
        k16 = jnp.maximum(k16, KEY_NEG_INF >> 16)
        return lax.bitcast_convert_type(((k16 ^ ((k16 >> 15) & 0x7FFF)) & 0xFFFF) << 16, _f32)

    def count16(cand):
        one, zero = jnp.ones((), _bf16), jnp.zeros((), _bf16)

        def step(c, acc):
            ks = pl.multiple_of(c * t, t)
            for u in range(t // rc16):
                acc = acc + jnp.where(s16_ref[pl.ds(ks + u * rc16, rc16), :] >= cand, one, zero)
            return acc
        acc = over_blocks(step, jnp.zeros((rc16, t), _bf16))
        return jnp.sum(acc.astype(_f32), axis=0, keepdims=True).astype(_i32)

    def bisect16(p, k16):
        cand = k16 + lax.shift_left(jnp.int32(1), 15 - p)
        ok = count16(key16_to_float(cand).astype(_bf16)) >= nsel
        return jnp.where(ok, cand, k16)

    k16 = lax.fori_loop(0, 16, bisect16, jnp.full((1, t), -2 ** 15, _i32))
    thr16 = lax.bitcast_convert_type(key16_to_float(k16), _i32)
    low = (thr16 ^ ((thr16 >> 31) & 0x7FFFFFFF)) - (2 ** 15 + 1)

    def bisect(p, st):
        key, cnt_key = st
        cand = key + lax.shift_left(jnp.int32(1), 16 - p)
        cand_f = key_to_float(cand)
        cnt = count(lambda blk: blk >= cand_f)
        ok = cnt >= nsel
        return jnp.where(ok, cand, key), jnp.where(ok, cnt, cnt_key)

    def pending(cnt_key):
        return jnp.max(jnp.where(cnt_key == nsel, 0, 1))

    def refine(st):
        p, _, key, cnt_key = st
        key, cnt_key = bisect(p, (key, cnt_key))
        return p + 1, pending(cnt_key), key, cnt_key

    key, cnt_key = lax.fori_loop(0, STAGE2_UNCHECKED, bisect, (low, jnp.zeros((1, t), _i32)))
    _, _, key, cnt_key = lax.while_loop(
        lambda st: jnp.logical_and(st[0] < 17, st[1] > 0), refine,
        (jnp.int32(STAGE2_UNCHECKED), pending(cnt_key), key, cnt_key))
    thr = key_to_float(key)

    tied = jnp.logical_and(cnt_key > nsel, thr > -jnp.inf)
    mx_ref[0:1, :] = jnp.maximum(thr, F32_LOWEST)

    @pl.when(jnp.max(jnp.where(tied, 1, 0)) > 0)
    def _():
        thr_up = key_to_float(key + 1)

        def relabel(c, _):
            ks = pl.multiple_of(c * t, t)
            for u in range(t // rc):
                rows = pl.ds(ks + u * rc, rc)
                x = sc_ref[rows, :]
                inside = jnp.logical_and(x >= thr, x > -jnp.inf)
                sc_ref[rows, :] = jnp.where(x >= thr_up, jnp.inf, jnp.where(inside, x - thr, -jnp.inf))
            return 0

        lax.fori_loop(0, nkb, relabel, 0)

        def bisect32(p, st):
            key_r, cnt_r = st
            cand = key_r + lax.shift_left(jnp.int32(1), 31 - p)
            cand_f = key_to_float(cand)
            cnt = count(lambda blk: blk >= cand_f)
            ok = cnt >= nsel
            return jnp.where(ok, cand, key_r), jnp.where(ok, cnt, cnt_r)

        key_r, _ = lax.fori_loop(
            0, 32, bisect32, (jnp.full((1, t), INT_MIN, _i32), jnp.full((1, t), nsel, _i32)))
        thr_r = key_to_float(key_r)
        need = (nsel - count(lambda blk: blk > thr_r)).astype(_f32)
        r_i = lax.broadcasted_iota(_i32, (t, t), 0)
        c_i = lax.broadcasted_iota(_i32, (t, t), 1)
        before = (c_i < r_i).astype(_bf16)

        def fix(c, run):
            ks = pl.multiple_of(c * t, t)
            blk = sc_ref[pl.ds(ks, t), :]
            eq = blk == thr_r
            eqf = jnp.where(eq, 1.0, 0.0)
            rank = _dot(before, eqf.astype(_bf16)) + run
            sc_ref[pl.ds(ks, t), :] = jnp.where(jnp.logical_and(eq, rank >= need), -jnp.inf, blk)
            return run + jnp.sum(eqf, axis=0, keepdims=True)

        lax.fori_loop(0, nkb, fix, jnp.zeros((1, t), _f32))
        mx_ref[0:1, :] = jnp.maximum(thr_r, F32_LOWEST)

    thr_sel = mx_ref[0:1, :]

    def to_mask(c, nearest):
        ks = pl.multiple_of(c * t, t)
        for u in range(t // rc):
            rows = pl.ds(ks + u * rc, rc)
            sel = sc_ref[rows, :] >= thr_sel
            sc_ref[rows, :] = jnp.where(sel, 0.0, MASKED)
            dist = jnp.abs(ks + u * rc + lax.broadcasted_iota(_i32, (rc, 1), 0) - qpos)
            nearest = jnp.minimum(nearest, jnp.where(sel, dist, FAR))
        return nearest

    nearest = lax.fori_loop(0, nkb, to_mask, jnp.full((rc, t), FAR, _i32))
    nearest = jnp.min(nearest, axis=0, keepdims=True).astype(_f32)

    wt = wide * t
    n_span = lax.shift_right_logical(i + wide, wide.bit_length() - 1)
    last = n_span - 1

    def mask_tail(c, _):
        sc_ref[pl.ds(pl.multiple_of(c * t, t), t), :] = jnp.full((t, t), MASKED, _f32)
        return 0

    lax.fori_loop(nkb, n_span * wide, mask_tail, 0)
    ahead_ref[...] = jnp.maximum(
        last * wt + lax.broadcasted_iota(_i32, (wt, 1), 0) - qpos, 0).astype(_f32)

    n_groups = SA_HEADS // hgrp

    @pl.when(i == 0)
    def _():
        rows = 2 * t
        for h in range(SA_HEADS):
            def longest(c, best):
                kk = k_ref[h, pl.ds(pl.multiple_of(c * rows, rows), rows), :].astype(_f32)
                return jnp.maximum(best, jnp.max(jnp.sum(kk * kk, axis=1, keepdims=True), axis=0, keepdims=True))
            best = lax.fori_loop(0, k_ref.shape[1] // rows, longest, jnp.zeros((1, 1), _f32))
            kn_ref[h:h + 1, :] = jnp.broadcast_to(best, (1, LANES))

    feat = lax.broadcasted_iota(_i32, (LANES, t), 0)
    a_t = (qpos >> 6).astype(_f32)
    b_t = (qpos & (CHUNK - 1)).astype(_f32)

    def set_queries(bound):
        for h in range(SA_HEADS):
            m_h = 2.0 ** -(h + 1)
            q_t = q_ref[h].astype(_f32).T
            extra = jnp.where(feat == 0, CHUNK * m_h,
                              jnp.where(feat == 1, m_h,
                                        jnp.where(feat == 2, -CHUNK * m_h * a_t,
                                                  jnp.where(feat == 3, -m_h * b_t, 0.0))))
            if bound is not None:
                extra = jnp.where(feat == 4, -bound(h, q_t), extra)
            qcat_ref[h] = jnp.concatenate([q_t, extra], axis=0).astype(_bf16)

    def logits(h, c, j, is_last):
        rows = pl.ds(pl.multiple_of(c * wt, wt) + j * t, t)
        lhs = jnp.concatenate([k_ref[h, rows, :], ka_ref[rows, :]], axis=1)
        lg = _dot(lhs, qcat_ref[h]) + sc_ref[rows, :]
        if is_last:
            two_m = lax.bitcast_convert_type(jnp.full((1, t), 127, _i32) - h << 23, _f32)
            lg = lg - two_m * ahead_ref[j * t:(j + 1) * t, :]
        return lg

    def attend():
        def head_group(hp, l_min):
            heads = tuple(hp + a * n_groups for a in range(hgrp))

            def span(c, states, is_last):
                l_run = [st[0] for st in states]
                acc = [st[1] for st in states]
                items = [(j, a) for j in range(wide) for a in range(hgrp)]
                ready = [logits(heads[a], c, j, is_last) for j, a in items[:ahead_items]]
                for n, (j, a) in enumerate(items):
                    if n + ahead_items < len(items):
                        jn, an = items[n + ahead_items]
                        ready.append(logits(heads[an], c, jn, is_last))
                    p = jnp.exp(ready[n])
                    l_run[a] = l_run[a] + jnp.sum(p, axis=0, keepdims=True)
                    acc[a] = acc[a] + _dot(vt_ref[heads[a], c * wide + j], p.astype(_bf16))
                return tuple(zip(l_run, acc))

            init = (jnp.zeros((1, t), _f32), jnp.zeros((HEAD_DIM, t), _f32))
            states = lax.fori_loop(0, last, lambda c, st: span(c, st, False), (init,) * hgrp)
            states = span(last, states, True)
            for h, (l_run, acc) in zip(heads, states):
                out = (acc * (1.0 / l_run)).T
                zg = z_ref[h].astype(_f32)
                o_ref[h] = (out * (zg / (1.0 + jnp.exp(-zg)))).astype(o_ref.dtype)
                l_min = jnp.minimum(l_min, l_run)
            return l_min

        return jnp.min(lax.fori_loop(0, n_groups, head_group, jnp.full((1, t), jnp.inf, _f32)))

    def logit_bound(h, q_t):
        reach = jnp.sqrt(jnp.sum(q_t * q_t, axis=0, keepdims=True) * kn_ref[h:h + 1, 0:1])
        return reach - 2.0 ** -(h + 1) * nearest

    set_queries(logit_bound)
    l_min = attend()

    @pl.when(jnp.logical_not(l_min > SOFTMAX_FLOOR))
    def _():
        set_queries(None)

        def head_max(h, _):
            def span_max(c, best, is_last):
                for j in range(wide):
                    best = jnp.maximum(best, jnp.max(logits(h, c, j, is_last), axis=0, keepdims=True))
                return best
            best = lax.fori_loop(0, last, lambda c, b: span_max(c, b, False), jnp.full((1, t), MASKED, _f32))
            mx_ref[pl.ds(h, 1), :] = span_max(last, best, True)
            return 0

        lax.fori_loop(0, SA_HEADS, head_max, 0)
        set_queries(lambda h, q_t: mx_ref[h:h + 1, :])
        attend()


def _sa_attention(proj, misc, kx, ka, vt, b, s, *, t=256, rc=64, rc16=128, sblk=2, wide=4, hgrp=4, ahead_items=4):
    m = b * s
    nq = s // t
    nsel = min(TOPK_MAX, s // 4)
    once = pl.Buffered(1)
    return pl.pallas_call(
        functools.partial(_sa_kernel, t=t, rc=rc, rc16=rc16, sblk=sblk, wide=wide, hgrp=hgrp, ahead_items=ahead_items, nsel=nsel),
        grid=(b, nq),
        in_specs=[
            pl.BlockSpec((8, t, LANES), lambda bi, i: (G_QIX // 8, bi * nq + i, 0)),
            pl.BlockSpec((t, LANES), lambda bi, i: (bi * nq + i, 0)),
            pl.BlockSpec((s, LANES), lambda bi, i: (bi, 0), pipeline_mode=once),
            pl.BlockSpec((8, t, LANES), lambda bi, i: (G_QSA // 8, bi * nq + i, 0)),
            pl.BlockSpec((8, s, LANES), lambda bi, i: (G_KSA // 8, bi, 0), pipeline_mode=once),
            pl.BlockSpec((s, LANES), lambda bi, i: (0, 0), pipeline_mode=once),
            pl.BlockSpec((8, s // t, LANES, t), lambda bi, i: (0, bi, 0, 0), pipeline_mode=once),
            pl.BlockSpec((8, t, LANES), lambda bi, i: (G_ZSA // 8, bi * nq + i, 0)),
        ],
        out_specs=pl.BlockSpec((8, t, LANES), lambda bi, i: (0, bi * nq + i, 0)),
        out_shape=jax.ShapeDtypeStruct((SA_HEADS, m, LANES), _bf16),
        scratch_shapes=[
            pltpu.VMEM((s, t), _f32),
            pltpu.VMEM((s, t), _bf16),
            pltpu.VMEM((IDX_HEADS, LANES, t), _bf16),
            pltpu.VMEM((SA_HEADS, 2 * LANES, t), _bf16),
            pltpu.VMEM((wide * t, t), _f32),
            pltpu.VMEM((SA_HEADS, LANES), _f32),
            pltpu.VMEM((SA_HEADS, t), _f32),
        ],
        compiler_params=pltpu.CompilerParams(
            dimension_semantics=("arbitrary", "arbitrary"), vmem_limit_bytes=VMEM_LIMIT),
        name="sa_attention",
    )(proj, misc, kx, proj, proj, ka, vt, proj)


def _sigmoid(v):
    return 1.0 / (1.0 + jnp.exp(-v))


def _out_kernel(asb_ref, asa_ref, gsb_ref, gsa_ref, x_ref, wsb_ref, wsa_ref, wo_ref, fg_ref,
                o_ref, *, final):
    asa = jnp.concatenate([asa_ref[h] for h in range(SA_HEADS)], axis=1)
    ysb = _dot(asb_ref[...], wsb_ref[...])
    ysa = _dot(asa, wsa_ref[...])
    n_g = D_MODEL // LANES
    gsb = jnp.concatenate([gsb_ref[g] for g in range(n_g)], axis=1).astype(_f32)
    gsa = jnp.concatenate([gsa_ref[g] for g in range(n_g)], axis=1).astype(_f32)
    mixed = _sigmoid(gsb) * ysb + _sigmoid(gsa) * ysa
    hres = x_ref[...] + _dot(mixed.astype(_bf16), wo_ref[...])
    if final:
        ms = jnp.mean(hres * hres, axis=-1, keepdims=True)
        hres = (hres * lax.rsqrt(ms + RMS_EPS)) * fg_ref[...]
    o_ref[...] = hres


def _out_projection(a_sb, a_sa, proj, x2, w_sb, w_sa, w_o, fgain, *, final, tm=512):
    m, d = x2.shape
    n_g = d // LANES
    once = pl.Buffered(1)
    return pl.pallas_call(
        functools.partial(_out_kernel, final=final),
        grid=(m // tm,),
        in_specs=[
            pl.BlockSpec((tm, SB_HEADS * HEAD_DIM), lambda i: (i, 0)),
            pl.BlockSpec((SA_HEADS, tm, LANES), lambda i: (0, i, 0)),
            pl.BlockSpec((n_g, tm, LANES), lambda i: (G_GSB // n_g, i, 0)),
            pl.BlockSpec((n_g, tm, LANES), lambda i: (G_GSA // n_g, i, 0)),
            pl.BlockSpec((tm, d), lambda i: (i, 0)),
            pl.BlockSpec(w_sb.shape, lambda i: (0, 0), pipeline_mode=once),
            pl.BlockSpec(w_sa.shape, lambda i: (0, 0), pipeline_mode=once),
            pl.BlockSpec(w_o.shape, lambda i: (0, 0), pipeline_mode=once),
            pl.BlockSpec((1, d), lambda i: (0, 0)),
        ],
        out_specs=pl.BlockSpec((tm, d), lambda i: (i, 0)),
        out_shape=jax.ShapeDtypeStruct((m, d), _f32),
        compiler_params=pltpu.CompilerParams(
            dimension_semantics=("arbitrary",), vmem_limit_bytes=VMEM_LIMIT),
        name="out_projection",
    )(a_sb, a_sa, proj, proj, x2, w_sb, w_sa, w_o, fgain)


def _split_in_weights(w):
    c_qix = 4 * SB_HEADS * HEAD_DIM + 4 * SA_HEADS * HEAD_DIM
    c_kix = c_qix + IDX_HEADS * IDX_DIM
    c_gsb = c_kix + IDX_DIM + IDX_HEADS
    w_lead = w[:, :c_kix].astype(_bf16)
    w_gate = w[:, c_gsb:].astype(_bf16)
    w_misc = jnp.pad(w[:, c_kix:c_gsb], ((0, 0), (0, LANES - (c_gsb - c_kix)))).astype(_bf16)
    return w_lead, w_gate, w_misc


def _alibi_key_columns(s):
    pos = jnp.arange(s, dtype=_i32)
    one = jnp.ones_like(pos)
    cols = jnp.stack([pos // CHUNK, pos % CHUNK, one, one, one], axis=1)
    return jnp.pad(cols, ((0, 0), (0, LANES - 5))).astype(_bf16)


def kernel(x, norm_gain, w_in, w_branch_sb, w_branch_sa, w_out, final_norm_gain):
    b, s, d = x.shape
    depth = norm_gain.shape[0]
    h2 = x.reshape(b * s, d)
    fgain = final_norm_gain.reshape(1, d)
    for layer in range(depth):
        w_lead, w_gate, w_misc = _split_in_weights(w_in[layer])
        proj, misc, vt = _in_projection(h2, norm_gain[layer].reshape(1, d), w_lead, w_gate, w_misc)
        kx = jnp.pad(misc[:, :IDX_DIM], ((0, 0), (0, LANES - IDX_DIM))).astype(_bf16)
        a_sb = _sb_attention(proj, b, s)
        a_sa = _sa_attention(proj, misc, kx, _alibi_key_columns(s), vt, b, s)
        h2 = _out_projection(a_sb, a_sa, proj, h2,
                             w_branch_sb[layer].astype(_bf16), w_branch_sa[layer].astype(_bf16),
                             w_out[layer].astype(_bf16), fgain, final=(layer == depth - 1))
    return h2.reshape(b, s, d)
```

```python
import functools

import jax
import jax.numpy as jnp
from jax import lax
from jax.experimental import pallas as pl
from jax.experimental.pallas import tpu as pltpu

D_MODEL = 2048
CHUNK = 64
SB_HEADS = 8
SA_HEADS = 8
HEAD_DIM = 128
IDX_HEADS = 16
IDX_DIM = 64
TOPK_MAX = 256
RMS_EPS = 1e-6

LANES = 128
VMEM_LIMIT = 60 * 1024 * 1024

G_QSB, G_KSB, G_VSB, G_ZSB = 0, 8, 16, 24
G_QSA, G_KSA, G_VSA, G_ZSA = 32, 40, 48, 56
G_GSB, G_GSA, G_QIX = 64, 80, 96
N_GROUPS = 104
GROUPS_PER_TILE = 8

INT_MIN = -2 ** 31
KEY_NEG_INF = INT_MIN + 0x7FFFFF
F32_LOWEST = -3.4028234663852886e38
MASKED = -1e30
ATTN_SCALE = HEAD_DIM ** -0.5
SB_EXIT = -104.0
SOFTMAX_FLOOR = 1e-17
FAR = 2 ** 30
STAGE2_UNCHECKED = 9

_f32 = jnp.float32
_bf16 = jnp.bfloat16
_i32 = jnp.int32


def _dot_nt(a, b):
    return lax.dot_general(a, b, (((1,), (1,)), ((), ())), preferred_element_type=_f32)


def _dot(a, b):
    return jnp.dot(a, b, preferred_element_type=_f32)


def _inproj_kernel(x_ref, g_ref, wl_ref, wg_ref, wm_ref, o_ref, misc_ref, vt_ref, u_ref,
                   *, tm, sub, dsub):
    n = pl.program_id(1)
    first_gate = G_GSB // GROUPS_PER_TILE
    is_gate = jnp.logical_and(n >= first_gate, n < G_QIX // GROUPS_PER_TILE)

    @pl.when(n == 0)
    def _():
        for r in range(tm // sub):
            x = x_ref[r * sub:(r + 1) * sub, :]
            ms = jnp.mean(x * x, axis=-1, keepdims=True)
            u = (x * lax.rsqrt(ms + RMS_EPS)) * g_ref[...]
            ub = u.astype(_bf16)
            u_ref[r * sub:(r + 1) * sub, :] = ub
            misc_ref[r * sub:(r + 1) * sub, :] = _dot(ub, wm_ref[...])

    is_query = jnp.logical_or(n == G_QSB // GROUPS_PER_TILE, n == G_QSA // GROUPS_PER_TILE)
    mult = jnp.where(is_query, ATTN_SCALE, 1.0).astype(_f32)

    def project(w_ref, with_vt):
        for r in range(tm // dsub):
            res = _dot(u_ref[r * dsub:(r + 1) * dsub, :], w_ref[...]) * mult
            for g in range(GROUPS_PER_TILE):
                o_ref[g, r * dsub:(r + 1) * dsub, :] = res[:, g * LANES:(g + 1) * LANES].astype(_bf16)
            if with_vt:
                @pl.when(n == G_VSA // GROUPS_PER_TILE)
                def _():
                    for g in range(GROUPS_PER_TILE):
                        for q in range(dsub // sub):
                            vt_ref[g, r * (dsub // sub) + q] = (
                                res[q * sub:(q + 1) * sub, g * LANES:(g + 1) * LANES].T.astype(_bf16))

    @pl.when(is_gate)
    def _():
        project(wg_ref, False)

    @pl.when(jnp.logical_not(is_gate))
    def _():
        project(wl_ref, True)


def _in_projection(x2, gain, w_lead, w_gate, w_misc, *, tm=1024, sub=256, dsub=1024):
    m, d = x2.shape
    n_tiles = N_GROUPS // GROUPS_PER_TILE
    tn = GROUPS_PER_TILE * LANES
    first_gate, n_gate = G_GSB // GROUPS_PER_TILE, w_gate.shape[1] // tn
    return pl.pallas_call(
        functools.partial(_inproj_kernel, tm=tm, sub=sub, dsub=dsub),
        grid=(m // tm, n_tiles),
        in_specs=[
            pl.BlockSpec((tm, d), lambda i, j: (i, 0)),
            pl.BlockSpec((1, d), lambda i, j: (0, 0)),
            pl.BlockSpec((d, tn), lambda i, j: (0, jnp.minimum(j, first_gate))),
            pl.BlockSpec((d, tn), lambda i, j: (0, jnp.clip(j - first_gate, 0, n_gate - 1))),
            pl.BlockSpec((d, LANES), lambda i, j: (0, 0)),
        ],
        out_specs=[
            pl.BlockSpec((GROUPS_PER_TILE, tm, LANES), lambda i, j: (j, i, 0)),
            pl.BlockSpec((tm, LANES), lambda i, j: (i, 0)),
            pl.BlockSpec((SA_HEADS, tm // sub, LANES, sub), lambda i, j: (0, i, 0, 0)),
        ],
        out_shape=[
            jax.ShapeDtypeStruct((N_GROUPS, m, LANES), _bf16),
            jax.ShapeDtypeStruct((m, LANES), _f32),
            jax.ShapeDtypeStruct((SA_HEADS, m // sub, LANES, sub), _bf16),
        ],
        scratch_shapes=[pltpu.VMEM((tm, d), _bf16)],
        compiler_params=pltpu.CompilerParams(
            dimension_semantics=("arbitrary", "arbitrary"), vmem_limit_bytes=VMEM_LIMIT),
        name="in_projection",
    )(x2, gain, w_lead, w_gate, w_misc)


def _sb_kernel(q_ref, k_ref, v_ref, z_ref, o_ref, *, t, hg):
    i = pl.program_id(2)
    row = lax.broadcasted_iota(_i32, (t, t), 0)
    col = lax.broadcasted_iota(_i32, (t, t), 1)
    tri = (row >= col).astype(_bf16)
    strict = col < row
    heads = range(hg)

    def scores(h, j):
        return _dot_nt(q_ref[h], k_ref[h, pl.ds(pl.multiple_of(j * t, t), t), :])

    def log_keep(z):
        return jnp.minimum(-z, 0.0) - jnp.log(1.0 + jnp.exp(-jnp.abs(z)))

    def suffix_sum(lk):
        hi = lk.astype(_bf16)
        lo = (lk - hi.astype(_f32)).astype(_bf16)
        return _dot(hi, tri) + _dot(lo, tri)

    def values(h, j):
        return v_ref[h, pl.ds(pl.multiple_of(j * t, t), t), :]

    has_prev = i > 0
    jp = jnp.maximum(i - 1, 0)
    z_d = [scores(h, i) for h in heads]
    z_p = [scores(h, jp) for h in heads]
    cum_d = [suffix_sum(jnp.where(strict, log_keep(z), 0.0)) for z in z_d]
    cum_p = [suffix_sum(jnp.where(has_prev, log_keep(z), 0.0)) for z in z_p]
    carry_d = [c[:, 0:1] for c in cum_d]
    w_d = [jnp.where(strict, jnp.exp(z + c), 0.0) for z, c in zip(z_d, cum_d)]
    w_p = [jnp.where(has_prev, jnp.exp(z + c + cr), 0.0) for z, c, cr in zip(z_p, cum_p, carry_d)]
    acc = [_dot(wd.astype(_bf16), values(h, i)) + _dot(wp.astype(_bf16), values(h, jp))
           for h, wd, wp in zip(heads, w_d, w_p)]
    carry = [cd + cp[:, 0:1] for cd, cp in zip(carry_d, cum_p)]

    def cond(st):
        j, carry, _ = st
        top = functools.reduce(jnp.maximum, carry)
        return jnp.logical_and(j >= 0, jnp.max(top) > SB_EXIT)

    def body(st):
        j, carry, acc = st
        z_j = [scores(h, j) for h in heads]
        cum = [suffix_sum(log_keep(z)) for z in z_j]
        w_j = [jnp.exp(z + c + cr) for z, c, cr in zip(z_j, cum, carry)]
        acc = tuple(a + _dot(w.astype(_bf16), values(h, j)) for h, a, w in zip(heads, acc, w_j))
        carry = tuple(cr + c[:, 0:1] for cr, c in zip(carry, cum))
        return j - 1, carry, acc

    _, _, acc = lax.while_loop(cond, body, (i - 2, tuple(carry), tuple(acc)))
    for h in heads:
        zg = z_ref[h].astype(_f32)
        o_ref[:, h * LANES:(h + 1) * LANES] = (acc[h] * (zg / (1.0 + jnp.exp(-zg)))).astype(o_ref.dtype)


def _sb_attention(proj, b, s, *, t=256, hg=4):
    m = b * s
    nq = s // t
    return pl.pallas_call(
        functools.partial(_sb_kernel, t=t, hg=hg),
        grid=(b, SB_HEADS // hg, nq),
        in_specs=[
            pl.BlockSpec((hg, t, LANES), lambda bi, g, i: (G_QSB // hg + g, bi * nq + i, 0)),
            pl.BlockSpec((hg, s, LANES), lambda bi, g, i: (G_KSB // hg + g, bi, 0)),
            pl.BlockSpec((hg, s, LANES), lambda bi, g, i: (G_VSB // hg + g, bi, 0)),
            pl.BlockSpec((hg, t, LANES), lambda bi, g, i: (G_ZSB // hg + g, bi * nq + i, 0)),
        ],
        out_specs=pl.BlockSpec((t, hg * LANES), lambda bi, g, i: (bi * nq + i, g)),
        out_shape=jax.ShapeDtypeStruct((m, SB_HEADS * HEAD_DIM), _bf16),
        compiler_params=pltpu.CompilerParams(
            dimension_semantics=("arbitrary", "arbitrary", "arbitrary"),
            vmem_limit_bytes=VMEM_LIMIT),
        name="sb_attention",
    )(proj, proj, proj, proj)


def _sa_kernel(qi_ref, wq_ref, kx_ref, q_ref, k_ref, ka_ref, vt_ref, z_ref, o_ref,
               sc_ref, s16_ref, qit_ref, qcat_ref, ahead_ref, kn_ref, mx_ref,
               *, t, rc, rc16, sblk, wide, hgrp, ahead_items, nsel):
    i = pl.program_id(1)
    nkb = i + 1
    qpos = i * t + lax.broadcasted_iota(_i32, (1, t), 1)
    vis_end = (qpos // CHUNK + 1) * CHUNK
    krow = lax.broadcasted_iota(_i32, (t, 1), 0)

    w_t = wq_ref[...].T
    w_rows = [w_t[IDX_DIM + h:IDX_DIM + h + 1, :] * (IDX_HEADS ** -0.5 * IDX_DIM ** -0.5)
              for h in range(IDX_HEADS)]
    for g in range(IDX_HEADS // 2):
        q_pair = qi_ref[g].astype(_f32).T
        qit_ref[2 * g] = q_pair.astype(_bf16)
        qit_ref[2 * g + 1] = jnp.concatenate([q_pair[IDX_DIM:], q_pair[:IDX_DIM]], axis=0).astype(_bf16)

    st = sblk * t
    srow = lax.broadcasted_iota(_i32, (st, 1), 0)

    def score_block(c, _):
        ks = pl.multiple_of(c * st, st)
        kxb = kx_ref[pl.ds(ks, st), :]
        acc = jnp.zeros((st, t), _f32)
        for h in range(IDX_HEADS):
            acc = acc + jnp.maximum(_dot(kxb, qit_ref[h]), 0.0) * w_rows[h]
        visible = jnp.where(ks + srow < vis_end, acc, -jnp.inf)
        sc_ref[pl.ds(ks, st), :] = visible
        s16_ref[pl.ds(ks, st), :] = visible.astype(_bf16)
        return 0

    lax.fori_loop(0, (nkb + sblk - 1) // sblk, score_block, 0)

    def key_to_float(key):
        key = jnp.maximum(key, KEY_NEG_INF)
        return lax.bitcast_convert_type(key ^ ((key >> 31) & 0x7FFFFFFF), _f32)

    def over_blocks(step, init):
        pairs = lax.shift_right_logical(nkb, 1)
        acc = lax.fori_loop(0, pairs, lambda c, acc: step(2 * c + 1, step(2 * c, acc)), init)
        return lax.fori_loop(2 * pairs, nkb, step, acc)

    def count(pred):
        def step(c, acc):
            ks = pl.multiple_of(c * t, t)
            for u in range(t // rc):
                acc = acc + jnp.where(pred(sc_ref[pl.ds(ks + u * rc, rc), :]), 1, 0)
            return acc
        return jnp.sum(over_blocks(step, jnp.zeros((rc, t), _i32)), axis=0, keepdims=True)

    def key16_to_float(k16):
        k16 = jnp.maximum(k16, KEY_NEG_INF >> 16)
        return lax.bitcast_convert_type(((k16 ^ ((k16 >> 15) & 0x7FFF)) & 0xFFFF) << 16, _f32)

    def count16(cand):
        one, zero = jnp.ones((), _bf16), jnp.zeros((), _bf16)

        def step(c, acc):
            ks = pl.multiple_of(c * t, t)
            for u in range(t // rc16):
                acc = acc + jnp.where(s16_ref[pl.ds(ks + u * rc16, rc16), :] >= cand, one, zero)
            return acc
        acc = over_blocks(step, jnp.zeros((rc16, t), _bf16))
        return jnp.sum(acc.astype(_f32), axis=0, keepdims=True).astype(_i32)

    def bisect16(p, k16):
        cand = k16 + lax.shift_left(jnp.int32(1), 15 - p)
        ok = count16(key16_to_float(cand).astype(_bf16)) >= nsel
        return jnp.where(ok, cand, k16)

    k16 = lax.fori_loop(0, 16, bisect16, jnp.full((1, t), -2 ** 15, _i32))
    thr16 = lax.bitcast_convert_type(key16_to_float(k16), _i32)
    low = (thr16 ^ ((thr16 >> 31) & 0x7FFFFFFF)) - (2 ** 15 + 1)

    def bisect(p, st):
        key, cnt_key = st
        cand = key + lax.shift_left(jnp.int32(1), 16 - p)
        cand_f = key_to_float(cand)
        cnt = count(lambda blk: blk >= cand_f)
        ok = cnt >= nsel
        return jnp.where(ok, cand, key), jnp.where(ok, cnt, cnt_key)

    def pending(cnt_key):
        return jnp.max(jnp.where(cnt_key == nsel, 0, 1))

    def refine(st):
        p, _, key, cnt_key = st
        key, cnt_key = bisect(p, (key, cnt_key))
        return p + 1, pending(cnt_key), key, cnt_key

    key, cnt_key = lax.fori_loop(0, STAGE2_UNCHECKED, bisect, (low, jnp.zeros((1, t), _i32)))
    _, _, key, cnt_key = lax.while_loop(
        lambda st: jnp.logical_and(st[0] < 17, st[1] > 0), refine,
        (jnp.int32(STAGE2_UNCHECKED), pending(cnt_key), key, cnt_key))
    thr = key_to_float(key)

    tied = jnp.logical_and(cnt_key > nsel, thr > -jnp.inf)
    mx_ref[0:1, :] = jnp.maximum(thr, F32_LOWEST)

    @pl.when(jnp.max(jnp.where(tied, 1, 0)) > 0)
    def _():
        thr_up = key_to_float(key + 1)

        def relabel(c, _):
            ks = pl.multiple_of(c * t, t)
            for u in range(t // rc):
                rows = pl.ds(ks + u * rc, rc)
                x = sc_ref[rows, :]
                inside = jnp.logical_and(x >= thr, x > -jnp.inf)
                sc_ref[rows, :] = jnp.where(x >= thr_up, jnp.inf, jnp.where(inside, x - thr, -jnp.inf))
            return 0

        lax.fori_loop(0, nkb, relabel, 0)

        def bisect32(p, st):
            key_r, cnt_r = st
            cand = key_r + lax.shift_left(jnp.int32(1), 31 - p)
            cand_f = key_to_float(cand)
            cnt = count(lambda blk: blk >= cand_f)
            ok = cnt >= nsel
            return jnp.where(ok, cand, key_r), jnp.where(ok, cnt, cnt_r)

        key_r, _ = lax.fori_loop(
            0, 32, bisect32, (jnp.full((1, t), INT_MIN, _i32), jnp.full((1, t), nsel, _i32)))
        thr_r = key_to_float(key_r)
        need = (nsel - count(lambda blk: blk > thr_r)).astype(_f32)
        r_i = lax.broadcasted_iota(_i32, (t, t), 0)
        c_i = lax.broadcasted_iota(_i32, (t, t), 1)
        before = (c_i < r_i).astype(_bf16)

        def fix(c, run):
            ks = pl.multiple_of(c * t, t)
            blk = sc_ref[pl.ds(ks, t), :]
            eq = blk == thr_r
            eqf = jnp.where(eq, 1.0, 0.0)
            rank = _dot(before, eqf.astype(_bf16)) + run
            sc_ref[pl.ds(ks, t), :] = jnp.where(jnp.logical_and(eq, rank >= need), -jnp.inf, blk)
            return run + jnp.sum(eqf, axis=0, keepdims=True)

        lax.fori_loop(0, nkb, fix, jnp.zeros((1, t), _f32))
        mx_ref[0:1, :] = jnp.maximum(thr_r, F32_LOWEST)

    thr_sel = mx_ref[0:1, :]

    def to_mask(c, nearest):
        ks = pl.multiple_of(c * t, t)
        for u in range(t // rc):
            rows = pl.ds(ks + u * rc, rc)
            sel = sc_ref[rows, :] >= thr_sel
            sc_ref[rows, :] = jnp.where(sel, 0.0, MASKED)
            dist = jnp.abs(ks + u * rc + lax.broadcasted_iota(_i32, (rc, 1), 0) - qpos)
            nearest = jnp.minimum(nearest, jnp.where(sel, dist, FAR))
        return nearest

    nearest = lax.fori_loop(0, nkb, to_mask, jnp.full((rc, t), FAR, _i32))
    nearest = jnp.min(nearest, axis=0, keepdims=True).astype(_f32)

    wt = wide * t
    n_span = lax.shift_right_logical(i + wide, wide.bit_length() - 1)
    last = n_span - 1

    def mask_tail(c, _):
        sc_ref[pl.ds(pl.multiple_of(c * t, t), t), :] = jnp.full((t, t), MASKED, _f32)
        return 0

    lax.fori_loop(nkb, n_span * wide, mask_tail, 0)
    ahead_ref[...] = jnp.maximum(
        last * wt + lax.broadcasted_iota(_i32, (wt, 1), 0) - qpos, 0).astype(_f32)

    n_groups = SA_HEADS // hgrp

    @pl.when(i == 0)
    def _():
        rows = 2 * t
        for h in range(SA_HEADS):
            def longest(c, best):
                kk = k_ref[h, pl.ds(pl.multiple_of(c * rows, rows), rows), :].astype(_f32)
                return jnp.maximum(best, jnp.max(jnp.sum(kk * kk, axis=1, keepdims=True), axis=0, keepdims=True))
            best = lax.fori_loop(0, k_ref.shape[1] // rows, longest, jnp.zeros((1, 1), _f32))
            kn_ref[h:h + 1, :] = jnp.broadcast_to(best, (1, LANES))

    feat = lax.broadcasted_iota(_i32, (LANES, t), 0)
    a_t = (qpos >> 6).astype(_f32)
    b_t = (qpos & (CHUNK - 1)).astype(_f32)

    def set_queries(bound):
        for h in range(SA_HEADS):
            m_h = 2.0 ** -(h + 1)
            q_t = q_ref[h].astype(_f32).T
            extra = jnp.where(feat == 0, CHUNK * m_h,
                              jnp.where(feat == 1, m_h,
                                        jnp.where(feat == 2, -CHUNK * m_h * a_t,
                                                  jnp.where(feat == 3, -m_h * b_t, 0.0))))
            if bound is not None:
                extra = jnp.where(feat == 4, -bound(h, q_t), extra)
            qcat_ref[h] = jnp.concatenate([q_t, extra], axis=0).astype(_bf16)

    def logits(h, c, j, is_last):
        rows = pl.ds(pl.multiple_of(c * wt, wt) + j * t, t)
        lhs = jnp.concatenate([k_ref[h, rows, :], ka_ref[rows, :]], axis=1)
        lg = _dot(lhs, qcat_ref[h]) + sc_ref[rows, :]
        if is_last:
            two_m = lax.bitcast_convert_type(jnp.full((1, t), 127, _i32) - h << 23, _f32)
            lg = lg - two_m * ahead_ref[j * t:(j + 1) * t, :]
        return lg

    def attend():
        def head_group(hp, l_min):
            heads = tuple(hp + a * n_groups for a in range(hgrp))

            def span(c, states, is_last):
                l_run = [st[0] for st in states]
                acc = [st[1] for st in states]
                items = [(j, a) for j in range(wide) for a in range(hgrp)]
                ready = [logits(heads[a], c, j, is_last) for j, a in items[:ahead_items]]
                for n, (j, a) in enumerate(items):
                    if n + ahead_items < len(items):
                        jn, an = items[n + ahead_items]
                        ready.append(logits(heads[an], c, jn, is_last))
                    p = jnp.exp(ready[n])
                    l_run[a] = l_run[a] + jnp.sum(p, axis=0, keepdims=True)
                    acc[a] = acc[a] + _dot(vt_ref[heads[a], c * wide + j], p.astype(_bf16))
                return tuple(zip(l_run, acc))

            init = (jnp.zeros((1, t), _f32), jnp.zeros((HEAD_DIM, t), _f32))
            states = lax.fori_loop(0, last, lambda c, st: span(c, st, False), (init,) * hgrp)
            states = span(last, states, True)
            for h, (l_run, acc) in zip(heads, states):
                out = (acc * (1.0 / l_run)).T
                zg = z_ref[h].astype(_f32)
                o_ref[h] = (out * (zg / (1.0 + jnp.exp(-zg)))).astype(o_ref.dtype)
                l_min = jnp.minimum(l_min, l_run)
            return l_min

        return jnp.min(lax.fori_loop(0, n_groups, head_group, jnp.full((1, t), jnp.inf, _f32)))

    def logit_bound(h, q_t):
        reach = jnp.sqrt(jnp.sum(q_t * q_t, axis=0, keepdims=True) * kn_ref[h:h + 1, 0:1])
        return reach - 2.0 ** -(h + 1) * nearest

    set_queries(logit_bound)
    l_min = attend()

    @pl.when(jnp.logical_not(l_min > SOFTMAX_FLOOR))
    def _():
        set_queries(None)

        def head_max(h, _):
            def span_max(c, best, is_last):
                for j in range(wide):
                    best = jnp.maximum(best, jnp.max(logits(h, c, j, is_last), axis=0, keepdims=True))
                return best
            best = lax.fori_loop(0, last, lambda c, b: span_max(c, b, False), jnp.full((1, t), MASKED, _f32))
            mx_ref[pl.ds(h, 1), :] = span_max(last, best, True)
            return 0

        lax.fori_loop(0, SA_HEADS, head_max, 0)
        set_queries(lambda h, q_t: mx_ref[h:h + 1, :])
        attend()


def _sa_attention(proj, misc, kx, ka, vt, b, s, *, t=256, rc=64, rc16=128, sblk=2, wide=4, hgrp=8, ahead_items=4):
    m = b * s
    nq = s // t
    nsel = min(TOPK_MAX, s // 4)
    once = pl.Buffered(1)
    return pl.pallas_call(
        functools.partial(_sa_kernel, t=t, rc=rc, rc16=rc16, sblk=sblk, wide=wide, hgrp=hgrp, ahead_items=ahead_items, nsel=nsel),
        grid=(b, nq),
        in_specs=[
            pl.BlockSpec((8, t, LANES), lambda bi, i: (G_QIX // 8, bi * nq + i, 0)),
            pl.BlockSpec((t, LANES), lambda bi, i: (bi * nq + i, 0)),
            pl.BlockSpec((s, LANES), lambda bi, i: (bi, 0), pipeline_mode=once),
            pl.BlockSpec((8, t, LANES), lambda bi, i: (G_QSA // 8, bi * nq + i, 0)),
            pl.BlockSpec((8, s, LANES), lambda bi, i: (G_KSA // 8, bi, 0), pipeline_mode=once),
            pl.BlockSpec((s, LANES), lambda bi, i: (0, 0), pipeline_mode=once),
            pl.BlockSpec((8, s // t, LANES, t), lambda bi, i: (0, bi, 0, 0), pipeline_mode=once),
            pl.BlockSpec((8, t, LANES), lambda bi, i: (G_ZSA // 8, bi * nq + i, 0)),
        ],
        out_specs=pl.BlockSpec((8, t, LANES), lambda bi, i: (0, bi * nq + i, 0)),
        out_shape=jax.ShapeDtypeStruct((SA_HEADS, m, LANES), _bf16),
        scratch_shapes=[
            pltpu.VMEM((s, t), _f32),
            pltpu.VMEM((s, t), _bf16),
            pltpu.VMEM((IDX_HEADS, LANES, t), _bf16),
            pltpu.VMEM((SA_HEADS, 2 * LANES, t), _bf16),
            pltpu.VMEM((wide * t, t), _f32),
            pltpu.VMEM((SA_HEADS, LANES), _f32),
            pltpu.VMEM((SA_HEADS, t), _f32),
        ],
        compiler_params=pltpu.CompilerParams(
            dimension_semantics=("arbitrary", "arbitrary"), vmem_limit_bytes=VMEM_LIMIT),
        name="sa_attention",
    )(proj, misc, kx, proj, proj, ka, vt, proj)


def _sigmoid(v):
    return 1.0 / (1.0 + jnp.exp(-v))


def _out_kernel(asb_ref, asa_ref, gsb_ref, gsa_ref, x_ref, wsb_ref, wsa_ref, wo_ref, fg_ref,
                o_ref, *, final):
    asa = jnp.concatenate([asa_ref[h] for h in range(SA_HEADS)], axis=1)
    ysb = _dot(asb_ref[...], wsb_ref[...])
    ysa = _dot(asa, wsa_ref[...])
    n_g = D_MODEL // LANES
    gsb = jnp.concatenate([gsb_ref[g] for g in range(n_g)], axis=1).astype(_f32)
    gsa = jnp.concatenate([gsa_ref[g] for g in range(n_g)], axis=1).astype(_f32)
    mixed = _sigmoid(gsb) * ysb + _sigmoid(gsa) * ysa
    hres = x_ref[...] + _dot(mixed.astype(_bf16), wo_ref[...])
    if final:
        ms = jnp.mean(hres * hres, axis=-1, keepdims=True)
        hres = (hres * lax.rsqrt(ms + RMS_EPS)) * fg_ref[...]
    o_ref[...] = hres


def _out_projection(a_sb, a_sa, proj, x2, w_sb, w_sa, w_o, fgain, *, final, tm=512):
    m, d = x2.shape
    n_g = d // LANES
    once = pl.Buffered(1)
    return pl.pallas_call(
        functools.partial(_out_kernel, final=final),
        grid=(m // tm,),
        in_specs=[
            pl.BlockSpec((tm, SB_HEADS * HEAD_DIM), lambda i: (i, 0)),
            pl.BlockSpec((SA_HEADS, tm, LANES), lambda i: (0, i, 0)),
            pl.BlockSpec((n_g, tm, LANES), lambda i: (G_GSB // n_g, i, 0)),
            pl.BlockSpec((n_g, tm, LANES), lambda i: (G_GSA // n_g, i, 0)),
            pl.BlockSpec((tm, d), lambda i: (i, 0)),
            pl.BlockSpec(w_sb.shape, lambda i: (0, 0), pipeline_mode=once),
            pl.BlockSpec(w_sa.shape, lambda i: (0, 0), pipeline_mode=once),
            pl.BlockSpec(w_o.shape, lambda i: (0, 0), pipeline_mode=once),
            pl.BlockSpec((1, d), lambda i: (0, 0)),
        ],
        out_specs=pl.BlockSpec((tm, d), lambda i: (i, 0)),
        out_shape=jax.ShapeDtypeStruct((m, d), _f32),
        compiler_params=pltpu.CompilerParams(
            dimension_semantics=("arbitrary",), vmem_limit_bytes=VMEM_LIMIT),
        name="out_projection",
    )(a_sb, a_sa, proj, proj, x2, w_sb, w_sa, w_o, fgain)


def _split_in_weights(w):
    c_qix = 4 * SB_HEADS * HEAD_DIM + 4 * SA_HEADS * HEAD_DIM
    c_kix = c_qix + IDX_HEADS * IDX_DIM
    c_gsb = c_kix + IDX_DIM + IDX_HEADS
    w_lead = w[:, :c_kix].astype(_bf16)
    w_gate = w[:, c_gsb:].astype(_bf16)
    w_misc = jnp.pad(w[:, c_kix:c_gsb], ((0, 0), (0, LANES - (c_gsb - c_kix)))).astype(_bf16)
    return w_lead, w_gate, w_misc


def _alibi_key_columns(s):
    pos = jnp.arange(s, dtype=_i32)
    one = jnp.ones_like(pos)
    cols = jnp.stack([pos // CHUNK, pos % CHUNK, one, one, one], axis=1)
    return jnp.pad(cols, ((0, 0), (0, LANES - 5))).astype(_bf16)


def kernel(x, norm_gain, w_in, w_branch_sb, w_branch_sa, w_out, final_norm_gain):
    b, s, d = x.shape
    depth = norm_gain.shape[0]
    h2 = x.reshape(b * s, d)
    fgain = final_norm_gain.reshape(1, d)
    for layer in range(depth):
        w_lead, w_gate, w_misc = _split_in_weights(w_in[layer])
        proj, misc, vt = _in_projection(h2, norm_gain[layer].reshape(1, d), w_lead, w_gate, w_misc)
        kx = jnp.pad(misc[:, :IDX_DIM], ((0, 0), (0, LANES - IDX_DIM))).astype(_bf16)
        a_sb = _sb_attention(proj, b, s)
        a_sa = _sa_attention(proj, misc, kx, _alibi_key_columns(s), vt, b, s)
        h2 = _out_projection(a_sb, a_sa, proj, h2,
                             w_branch_sb[layer].astype(_bf16), w_branch_sa[layer].astype(_bf16),
                             w_out[layer].astype(_bf16), fgain, final=(layer == depth - 1))
    return h2.reshape(b, s, d)
```
